```python
import math
import jax, jax.numpy as jnp
from jax import lax
import numpy as np

D_MODEL = 2048
BATCH = 16
SEQ = 2048
DEPTH = 1
DEC_BATCH = 2
DEC_SEQ = 4096
PAST_LEN = 128

D_CONV = 1024
D_HYENA = 1024
SHORT_K = 3
FILTER_EMB = 33
FILTER_ORDER = 64
DECAY_FAST = 0.3
DECAY_SLOW = 1.5
DECAY_TARGET = 1e-2
N_EXPERTS = 16
CAPACITY_FACTOR = 2
D_EXPERT = 5632
EPS = 1e-6
D_IN = 3 * D_CONV + 3 * D_HYENA + 2 * D_MODEL

kernel_name = "hybrid_shortconv_hyena_ecmoe_encoder"


def rmsnorm(x, g):
    xf = x.astype(jnp.float32)
    r = lax.rsqrt(jnp.mean(xf * xf, axis=-1, keepdims=True) + EPS)
    return (xf * r).astype(x.dtype) * g


def short_conv(x, w):
    L = x.shape[1]
    pad = SHORT_K // 2
    xp = jnp.pad(x, ((0, 0), (pad, pad), (0, 0)))
    y = xp[:, 0:L] * w[0]
    for k in range(1, SHORT_K):
        y = y + xp[:, k:k + L] * w[k]
    return y


def hyena_filters(L, w1, b1, w2, b2, w3, b3, w_fout, freq):
    f32 = jnp.float32
    t = jnp.linspace(0.0, 1.0, L, dtype=f32)[:, None]
    bands = (FILTER_EMB - 1) // 2
    ang = (2.0 * math.pi / L) * jnp.arange(L, dtype=f32)[:, None] * \
        jnp.linspace(1e-4, bands - 1, bands, dtype=f32)[None, :]
    z = jnp.concatenate([t, jnp.cos(ang), -jnp.sin(ang)], axis=-1)
    fr = freq.astype(f32)
    h = jnp.sin(fr * (z @ w1.astype(f32) + b1.astype(f32)))
    h = jnp.sin(fr * (h @ w2.astype(f32) + b2.astype(f32)))
    h = jnp.sin(fr * (h @ w3.astype(f32) + b3.astype(f32)))
    h = h @ w_fout.astype(f32)
    max_decay = math.log(DECAY_TARGET) / DECAY_FAST
    min_decay = math.log(DECAY_TARGET) / DECAY_SLOW
    deltas = jnp.abs(jnp.linspace(min_decay, max_decay, D_HYENA, dtype=f32))
    decay = jnp.exp(-t * deltas)
    h_fwd = h[:, :D_HYENA] * decay
    h_bwd = h[:, D_HYENA:] * decay
    return jnp.concatenate([h_fwd, jnp.zeros((1, D_HYENA), f32), h_bwd[:0:-1]], axis=0)


def bidir_fftconv(u, k, skip):
    L = u.shape[1]
    uf32 = u.astype(jnp.float32)
    uf = jnp.fft.rfft(uf32, n=2 * L, axis=1)
    kf = jnp.fft.rfft(k, n=2 * L, axis=0)
    y = jnp.fft.irfft(uf * kf[None], n=2 * L, axis=1)[:, :L]
    return (y + uf32 * skip.astype(jnp.float32)).astype(u.dtype)


def mixer(xn, w_in, conv_a_w, conv_h_w, filt_w1, filt_b1, filt_w2, filt_b2, filt_w3, filt_b3,
          filt_w_out, filt_freq, hyena_skip, w_proj_a, w_proj_h, w_out):
    L = xn.shape[1]
    proj = jnp.einsum('bld,de->ble', xn, w_in)
    a_in = proj[..., :3 * D_CONV]
    h_in = proj[..., 3 * D_CONV:3 * D_CONV + 3 * D_HYENA]
    gates = jax.nn.sigmoid(proj[..., 3 * D_CONV + 3 * D_HYENA:])
    bg, cg, xa = jnp.split(a_in, 3, axis=-1)
    ya = cg * short_conv(bg * xa, conv_a_w)
    hc = short_conv(h_in, conv_h_w)
    x0, x1, v = jnp.split(hc, 3, axis=-1)
    k = hyena_filters(L, filt_w1, filt_b1, filt_w2, filt_b2, filt_w3, filt_b3, filt_w_out, filt_freq)
    yb = x0 * bidir_fftconv(v * x1, k, hyena_skip)
    ga, gb = jnp.split(gates, 2, axis=-1)
    merged = ga * jnp.einsum('blc,cd->bld', ya, w_proj_a) + gb * jnp.einsum('blc,cd->bld', yb, w_proj_h)
    return jnp.einsum('bld,de->ble', merged, w_out)


def ec_moe(xn, w_router, w_gate, w_up, w_down):
    B, L, D = xn.shape
    T = B * L
    cap = max(1, CAPACITY_FACTOR * T // N_EXPERTS)
    xf = xn.reshape(T, D)
    aff = jax.nn.softmax((xf @ w_router).astype(jnp.float32), axis=-1)
    g, idx = lax.top_k(aff.T, cap)
    xe = xf[idx]
    h = jax.nn.silu(jnp.einsum('ecd,edf->ecf', xe, w_gate)) * jnp.einsum('ecd,edf->ecf', xe, w_up)
    ye = jnp.einsum('ecf,efd->ecd', h, w_down) * g[..., None].astype(xn.dtype)
    out = jnp.zeros((T, D), xn.dtype).at[idx.reshape(-1)].add(ye.reshape(-1, D))
    return out.reshape(B, L, D)


def trunk(x, w_in, conv_a_w, conv_h_w, filt_w1, filt_b1, filt_w2, filt_b2, filt_w3, filt_b3,
          filt_w_out, filt_freq, hyena_skip, w_proj_a, w_proj_h, w_out, norm_mix, norm_ffn,
          w_router, w_gate, w_up, w_down, norm_final):
    h = x
    for l in range(DEPTH):
        h = h + mixer(rmsnorm(h, norm_mix[l]), w_in[l], conv_a_w[l], conv_h_w[l],
                      filt_w1[l], filt_b1[l], filt_w2[l], filt_b2[l], filt_w3[l], filt_b3[l],
                      filt_w_out[l], filt_freq[l], hyena_skip[l], w_proj_a[l], w_proj_h[l], w_out[l])
        h = h + ec_moe(rmsnorm(h, norm_ffn[l]), w_router[l], w_gate[l], w_up[l], w_down[l])
    return rmsnorm(h, norm_final)


def setup_inputs(seed: int = 0) -> dict:
    key = jax.random.key(seed)
    ks = jax.random.split(key, 24)
    f32 = jnp.float32

    def nrm(k, shape, scale):
        return jax.random.normal(k, shape, f32) * scale

    def gain(k, shape):
        return 1.0 + 0.02 * jax.random.normal(k, shape, f32)

    Ld = DEPTH
    return {
        "x_prompt": nrm(ks[0], (BATCH, SEQ, D_MODEL), 1.0),
        "x_sample": nrm(ks[1], (DEC_BATCH, DEC_SEQ, D_MODEL), 1.0),
        "w_in": nrm(ks[2], (Ld, D_MODEL, D_IN), D_MODEL ** -0.5),
        "conv_a_w": nrm(ks[3], (Ld, SHORT_K, D_CONV), SHORT_K ** -0.5),
        "conv_h_w": nrm(ks[4], (Ld, SHORT_K, 3 * D_HYENA), SHORT_K ** -0.5),
        "filt_w1": nrm(ks[5], (Ld, FILTER_EMB, FILTER_ORDER), FILTER_EMB ** -0.5),
        "filt_b1": nrm(ks[6], (Ld, FILTER_ORDER), 0.02),
        "filt_w2": nrm(ks[7], (Ld, FILTER_ORDER, FILTER_ORDER), FILTER_ORDER ** -0.5),
        "filt_b2": nrm(ks[8], (Ld, FILTER_ORDER), 0.02),
        "filt_w3": nrm(ks[9], (Ld, FILTER_ORDER, FILTER_ORDER), FILTER_ORDER ** -0.5),
        "filt_b3": nrm(ks[10], (Ld, FILTER_ORDER), 0.02),
        "filt_w_out": nrm(ks[11], (Ld, FILTER_ORDER, 2 * D_HYENA), 0.1 * FILTER_ORDER ** -0.5),
        "filt_freq": gain(ks[12], (Ld, FILTER_ORDER)),
        "hyena_skip": nrm(ks[13], (Ld, D_HYENA), 1.0),
        "w_proj_a": nrm(ks[14], (Ld, D_CONV, D_MODEL), D_CONV ** -0.5),
        "w_proj_h": nrm(ks[15], (Ld, D_HYENA, D_MODEL), D_HYENA ** -0.5),
        "w_out": nrm(ks[16], (Ld, D_MODEL, D_MODEL), D_MODEL ** -0.5),
        "norm_mix": gain(ks[17], (Ld, D_MODEL)),
        "norm_ffn": gain(ks[18], (Ld, D_MODEL)),
        "w_router": nrm(ks[19], (Ld, D_MODEL, N_EXPERTS), D_MODEL ** -0.5),
        "w_gate": nrm(ks[20], (Ld, N_EXPERTS, D_MODEL, D_EXPERT), D_MODEL ** -0.5),
        "w_up": nrm(ks[21], (Ld, N_EXPERTS, D_MODEL, D_EXPERT), D_MODEL ** -0.5),
        "w_down": nrm(ks[22], (Ld, N_EXPERTS, D_EXPERT, D_MODEL), D_EXPERT ** -0.5),
        "norm_final": gain(ks[23], (D_MODEL,)),
    }


def reference(x_prompt, x_sample, w_in, conv_a_w, conv_h_w, filt_w1, filt_b1, filt_w2, filt_b2,
              filt_w3, filt_b3, filt_w_out, filt_freq, hyena_skip, w_proj_a, w_proj_h, w_out,
              norm_mix, norm_ffn, w_router, w_gate, w_up, w_down, norm_final):
    y_prompt = trunk(x_prompt, w_in, conv_a_w, conv_h_w, filt_w1, filt_b1, filt_w2, filt_b2,
                     filt_w3, filt_b3, filt_w_out, filt_freq, hyena_skip, w_proj_a, w_proj_h, w_out,
                     norm_mix, norm_ffn, w_router, w_gate, w_up, w_down, norm_final)
    y_sample = trunk(x_sample, w_in, conv_a_w, conv_h_w, filt_w1, filt_b1, filt_w2, filt_b2,
                     filt_w3, filt_b3, filt_w_out, filt_freq, hyena_skip, w_proj_a, w_proj_h, w_out,
                     norm_mix, norm_ffn, w_router, w_gate, w_up, w_down, norm_final)
    return (y_prompt, y_sample)
```

```python
import functools
import math

import jax
import jax.numpy as jnp
from jax import lax
from jax.experimental import pallas as pl
from jax.experimental.pallas import tpu as pltpu

D_MODEL = 2048
D_CONV = 1024
D_HYENA = 1024
SHORT_K = 3
FILTER_EMB = 33
DECAY_FAST = 0.3
DECAY_SLOW = 1.5
DECAY_TARGET = 1e-2
N_EXPERTS = 16
CAPACITY_FACTOR = 2
D_EXPERT = 5632
EPS = 1e-6
D_IN = 3 * D_CONV + 3 * D_HYENA + 2 * D_MODEL

VMEM_LIMIT_BYTES = 56 * 1024 * 1024


def _rms(x, g):
    r = lax.rsqrt(jnp.mean(x * x, axis=-1, keepdims=True) + EPS)
    return (x * r) * g


def _norm_proj_body(x_ref, g_ref, w_ref, o_ref, xn_ref):
    @pl.when(pl.program_id(1) == 0)
    def _():
        xn_ref[...] = _rms(x_ref[...], g_ref[...]).astype(jnp.bfloat16)

    o_ref[...] = jnp.dot(xn_ref[...], w_ref[...], preferred_element_type=jnp.float32)


def _norm_proj(x, g, w_bf16, tm=1024, tn=1024):
    T, D = x.shape
    N = w_bf16.shape[1]
    return pl.pallas_call(
        _norm_proj_body,
        grid=(T // tm, N // tn),
        in_specs=[
            pl.BlockSpec((tm, D), lambda i, j: (i, 0)),
            pl.BlockSpec((1, D), lambda i, j: (0, 0)),
            pl.BlockSpec((D, tn), lambda i, j: (0, j)),
        ],
        out_specs=pl.BlockSpec((tm, tn), lambda i, j: (i, j)),
        out_shape=jax.ShapeDtypeStruct((T, N), jnp.float32),
        scratch_shapes=[pltpu.VMEM((tm, D), jnp.bfloat16)],
        compiler_params=pltpu.CompilerParams(
            dimension_semantics=("parallel", "arbitrary"),
            vmem_limit_bytes=VMEM_LIMIT_BYTES),
        name="norm_proj",
    )(x, g.reshape(1, D), w_bf16)


def _merge_body(ya_ref, yb_ref, gl_ref, x_ref, wa_ref, wh_ref, wo_ref, gn_ref, wr_ref,
                h_ref, xn_ref, lg_ref):
    pa = jnp.dot(ya_ref[...].astype(jnp.bfloat16), wa_ref[...], preferred_element_type=jnp.float32)
    ph = jnp.dot(yb_ref[...].astype(jnp.bfloat16), wh_ref[...], preferred_element_type=jnp.float32)
    gates = jax.nn.sigmoid(gl_ref[...])
    merged = gates[:, :D_MODEL] * pa + gates[:, D_MODEL:] * ph
    out = jnp.dot(merged.astype(jnp.bfloat16), wo_ref[...], preferred_element_type=jnp.float32)
    h = x_ref[...] + out
    h_ref[...] = h
    xn = _rms(h, gn_ref[...])
    xn_ref[...] = xn
    lg_ref[...] = jnp.dot(xn, wr_ref[...], preferred_element_type=jnp.float32,
                          precision=lax.Precision.HIGHEST)


def _merge_out(ya, yb, gate_logits, x, wa, wh, wo, g_ffn, w_router, tm=256):
    T, D = x.shape
    const = lambda i: (0, 0)
    row = lambda i: (i, 0)
    return pl.pallas_call(
        _merge_body,
        grid=(T // tm,),
        in_specs=[
            pl.BlockSpec((tm, D_CONV), row),
            pl.BlockSpec((tm, D_HYENA), row),
            pl.BlockSpec((tm, 2 * D), row),
            pl.BlockSpec((tm, D), row),
            pl.BlockSpec((D_CONV, D), const, pipeline_mode=pl.Buffered(1)),
            pl.BlockSpec((D_HYENA, D), const, pipeline_mode=pl.Buffered(1)),
            pl.BlockSpec((D, D), const, pipeline_mode=pl.Buffered(1)),
            pl.BlockSpec((1, D), const, pipeline_mode=pl.Buffered(1)),
            pl.BlockSpec((D, N_EXPERTS), const, pipeline_mode=pl.Buffered(1)),
        ],
        out_specs=[
            pl.BlockSpec((tm, D), row),
            pl.BlockSpec((tm, D), row),
            pl.BlockSpec((tm, N_EXPERTS), row),
        ],
        out_shape=[
            jax.ShapeDtypeStruct((T, D), jnp.float32),
            jax.ShapeDtypeStruct((T, D), jnp.float32),
            jax.ShapeDtypeStruct((T, N_EXPERTS), jnp.float32),
        ],
        compiler_params=pltpu.CompilerParams(
            dimension_semantics=("parallel",),
            vmem_limit_bytes=VMEM_LIMIT_BYTES),
        name="merge_out",
    )(ya, yb, gate_logits, x, wa, wh, wo, g_ffn.reshape(1, D), w_router)


def _expert_body(x_ref, g_ref, wg_ref, wu_ref, wd_ref, o_ref):
    f = pl.program_id(2)
    x = x_ref[0]
    a = jnp.dot(x, wg_ref[0].astype(jnp.bfloat16), preferred_element_type=jnp.float32)
    b = jnp.dot(x, wu_ref[0].astype(jnp.bfloat16), preferred_element_type=jnp.float32)
    hmid = (a * jax.nn.sigmoid(a) * b).astype(jnp.bfloat16)
    part = jnp.dot(hmid, wd_ref[0].astype(jnp.bfloat16), preferred_element_type=jnp.float32)

    @pl.when(f == 0)
    def _():
        o_ref[0] = part

    @pl.when(f > 0)
    def _():
        o_ref[0] += part

    @pl.when(f == pl.num_programs(2) - 1)
    def _():
        o_ref[0] = o_ref[0] * g_ref[0]


def _expert_ffn(xe_bf16, g, w_gate, w_up, w_down, tf=256):
    E, C, D = xe_bf16.shape
    F = w_gate.shape[-1]
    tm = min(C, 1024)
    return pl.pallas_call(
        _expert_body,
        grid=(E, C // tm, F // tf),
        in_specs=[
            pl.BlockSpec((1, tm, D), lambda e, i, f: (e, i, 0)),
            pl.BlockSpec((1, tm, 1), lambda e, i, f: (e, i, 0)),
            pl.BlockSpec((1, D, tf), lambda e, i, f: (e, 0, f)),
            pl.BlockSpec((1, D, tf), lambda e, i, f: (e, 0, f)),
            pl.BlockSpec((1, tf, D), lambda e, i, f: (e, f, 0)),
        ],
        out_specs=pl.BlockSpec((1, tm, D), lambda e, i, f: (e, i, 0)),
        out_shape=jax.ShapeDtypeStruct((E, C, D), jnp.float32),
        compiler_params=pltpu.CompilerParams(
            dimension_semantics=("parallel", "parallel", "arbitrary"),
            vmem_limit_bytes=VMEM_LIMIT_BYTES),
        name="expert_ffn",
    )(xe_bf16, g.reshape(E, C, 1), w_gate, w_up, w_down)


def _final_norm_body(x_ref, g_ref, o_ref):
    o_ref[...] = _rms(x_ref[...], g_ref[...])


def _final_norm(x, g, tm=512):
    T, D = x.shape
    return pl.pallas_call(
        _final_norm_body,
        grid=(T // tm,),
        in_specs=[pl.BlockSpec((tm, D), lambda i: (i, 0)),
                  pl.BlockSpec((1, D), lambda i: (0, 0))],
        out_specs=pl.BlockSpec((tm, D), lambda i: (i, 0)),
        out_shape=jax.ShapeDtypeStruct((T, D), jnp.float32),
        compiler_params=pltpu.CompilerParams(dimension_semantics=("parallel",)),
        name="final_norm",
    )(x, g.reshape(1, D))


def _short_conv(x, w):
    L = x.shape[1]
    pad = SHORT_K // 2
    xp = jnp.pad(x, ((0, 0), (pad, pad), (0, 0)))
    y = xp[:, 0:L] * w[0]
    for k in range(1, SHORT_K):
        y = y + xp[:, k:k + L] * w[k]
    return y


def _hyena_filters(L, w1, b1, w2, b2, w3, b3, w_fout, freq):
    f32 = jnp.float32
    hp = lax.Precision.HIGHEST
    t = jnp.linspace(0.0, 1.0, L, dtype=f32)[:, None]
    bands = (FILTER_EMB - 1) // 2
    ang = (2.0 * math.pi / L) * jnp.arange(L, dtype=f32)[:, None] * \
        jnp.linspace(1e-4, bands - 1, bands, dtype=f32)[None, :]
    z = jnp.concatenate([t, jnp.cos(ang), -jnp.sin(ang)], axis=-1)
    h = jnp.sin(freq * (jnp.dot(z, w1, precision=hp) + b1))
    h = jnp.sin(freq * (jnp.dot(h, w2, precision=hp) + b2))
    h = jnp.sin(freq * (jnp.dot(h, w3, precision=hp) + b3))
    h = jnp.dot(h, w_fout, precision=hp)
    max_decay = math.log(DECAY_TARGET) / DECAY_FAST
    min_decay = math.log(DECAY_TARGET) / DECAY_SLOW
    deltas = jnp.abs(jnp.linspace(min_decay, max_decay, D_HYENA, dtype=f32))
    decay = jnp.exp(-t * deltas)
    h_fwd = h[:, :D_HYENA] * decay
    h_bwd = h[:, D_HYENA:] * decay
    return jnp.concatenate([h_fwd, jnp.zeros((1, D_HYENA), f32), h_bwd[:0:-1]], axis=0)


def _bidir_fftconv(u, k, skip):
    L = u.shape[1]
    uf = jnp.fft.rfft(u, n=2 * L, axis=1)
    kf = jnp.fft.rfft(k, n=2 * L, axis=0)
    y = jnp.fft.irfft(uf * kf[None], n=2 * L, axis=1)[:, :L]
    return y + u * skip


def _trunk(x, p):
    B, L, D = x.shape
    T = B * L
    xf = x.reshape(T, D)
    proj = _norm_proj(xf, p["norm_mix"], p["w_in"]).reshape(B, L, D_IN)
    a_in = proj[..., :3 * D_CONV]
    h_in = proj[..., 3 * D_CONV:3 * D_CONV + 3 * D_HYENA]
    gate_logits = proj[..., 3 * D_CONV + 3 * D_HYENA:]
    bg, cg, xa = jnp.split(a_in, 3, axis=-1)
    ya = cg * _short_conv(bg * xa, p["conv_a_w"])
    hc = _short_conv(h_in, p["conv_h_w"])
    x0, x1, v = jnp.split(hc, 3, axis=-1)
    k = _hyena_filters(L, p["filt_w1"], p["filt_b1"], p["filt_w2"], p["filt_b2"], p["filt_w3"],
                       p["filt_b3"], p["filt_w_out"], p["filt_freq"])
    yb = x0 * _bidir_fftconv(v * x1, k, p["hyena_skip"])

    h, xn, logits = _merge_out(ya.reshape(T, D_CONV), yb.reshape(T, D_HYENA),
                               gate_logits.reshape(T, 2 * D), xf, p["w_proj_a"], p["w_proj_h"],
                               p["w_out"], p["norm_ffn"], p["w_router"])

    cap = max(1, CAPACITY_FACTOR * T // N_EXPERTS)
    aff = jax.nn.softmax(logits, axis=-1)
    g, idx = lax.top_k(aff.T, cap)
    xe = xn.astype(jnp.bfloat16)[idx]
    ye = _expert_ffn(xe, g, p["w_gate"], p["w_up"], p["w_down"])
    h2 = h.at[idx.reshape(-1)].add(ye.reshape(-1, D))
    return _final_norm(h2, p["norm_final"]).reshape(B, L, D)


def kernel(x_prompt, x_sample, w_in, conv_a_w, conv_h_w, filt_w1, filt_b1, filt_w2, filt_b2, filt_w3, filt_b3, filt_w_out, filt_freq, hyena_skip, w_proj_a, w_proj_h, w_out, norm_mix, norm_ffn, w_router, w_gate, w_up, w_down, norm_final):
    bf16 = jnp.bfloat16
    p = dict(
        w_in=w_in[0].astype(bf16), conv_a_w=conv_a_w[0], conv_h_w=conv_h_w[0],
        filt_w1=filt_w1[0], filt_b1=filt_b1[0], filt_w2=filt_w2[0], filt_b2=filt_b2[0],
        filt_w3=filt_w3[0], filt_b3=filt_b3[0], filt_w_out=filt_w_out[0], filt_freq=filt_freq[0],
        hyena_skip=hyena_skip[0], w_proj_a=w_proj_a[0].astype(bf16), w_proj_h=w_proj_h[0].astype(bf16),
        w_out=w_out[0].astype(bf16), norm_mix=norm_mix[0], norm_ffn=norm_ffn[0],
        w_router=w_router[0], w_gate=w_gate[0], w_up=w_up[0], w_down=w_down[0],
        norm_final=norm_final,
    )
    return (_trunk(x_prompt, p), _trunk(x_sample, p))
```

```python
import functools
import math

import numpy as np

import jax
import jax.numpy as jnp
from jax import lax
from jax.experimental import pallas as pl
from jax.experimental.pallas import tpu as pltpu

D_MODEL = 2048
D_CONV = 1024
D_HYENA = 1024
SHORT_K = 3
FILTER_EMB = 33
FILTER_ORDER = 64
DECAY_FAST = 0.3
DECAY_SLOW = 1.5
DECAY_TARGET = 1e-2
N_EXPERTS = 16
CAPACITY_FACTOR = 2
D_EXPERT = 5632
EPS = 1e-6
D_IN = 3 * D_CONV + 3 * D_HYENA + 2 * D_MODEL
H_OFF = 3 * D_CONV
G_OFF = 3 * D_CONV + 3 * D_HYENA

LANES = 128
SUBLANES = 8
VMEM_LIMIT_BYTES = 56 * 1024 * 1024

_HP = lax.Precision.HIGHEST


def _rms(x, g):
    r = lax.rsqrt(jnp.mean(x * x, axis=-1, keepdims=True) + EPS)
    return (x * r) * g


def _bdot(a, b):
    return jnp.dot(a, b, preferred_element_type=jnp.float32)


def _params(*sem):
    return pltpu.CompilerParams(dimension_semantics=sem, vmem_limit_bytes=VMEM_LIMIT_BYTES)


def _norm_proj_body(x_ref, g_ref, w_ref, o_ref, xn_ref):
    @pl.when(pl.program_id(1) == 0)
    def _():
        xn_ref[...] = _rms(x_ref[...], g_ref[...]).astype(jnp.bfloat16)

    o_ref[...] = _bdot(xn_ref[...], w_ref[...])


def _norm_proj(x, g, w_bf16, tm=1024, tn=1024):
    T, D = x.shape
    N = w_bf16.shape[1]
    return pl.pallas_call(
        _norm_proj_body,
        grid=(T // tm, N // tn),
        in_specs=[
            pl.BlockSpec((tm, D), lambda i, j: (i, 0)),
            pl.BlockSpec((1, D), lambda i, j: (0, 0)),
            pl.BlockSpec((D, tn), lambda i, j: (0, j)),
        ],
        out_specs=pl.BlockSpec((tm, tn), lambda i, j: (i, j)),
        out_shape=jax.ShapeDtypeStruct((T, N), jnp.float32),
        scratch_shapes=[pltpu.VMEM((tm, D), jnp.bfloat16)],
        compiler_params=_params("parallel", "arbitrary"),
        name="norm_proj",
    )(x, g.reshape(1, D), w_bf16)


def _shift_down(x, first_row):
    rows = lax.broadcasted_iota(jnp.int32, x.shape, 0)
    return jnp.where(rows == 0, first_row, pltpu.roll(x, 1, axis=0))


def _shift_up(x, last_row):
    n = x.shape[0]
    rows = lax.broadcasted_iota(jnp.int32, x.shape, 0)
    return jnp.where(rows == n - 1, last_row, pltpu.roll(x, n - 1, axis=0))


def _mix_body(tiles_per_seq, p_ref, pp_ref, pn_ref, wa_ref, wh_ref,
              ya_ref, x0_ref, u_ref, ub_ref):
    i = pl.program_id(0)
    keep_prev = jnp.where(i % tiles_per_seq == 0, 0.0, 1.0)
    keep_next = jnp.where(i % tiles_per_seq == tiles_per_seq - 1, 0.0, 1.0)
    C = D_CONV
    q = p_ref[:, 0:C] * p_ref[:, 2 * C:3 * C]
    q_prev = pp_ref[SUBLANES - 1:SUBLANES, 0:C] * pp_ref[SUBLANES - 1:SUBLANES, 2 * C:3 * C] * keep_prev
    q_next = pn_ref[0:1, 0:C] * pn_ref[0:1, 2 * C:3 * C] * keep_next
    conv = (_shift_down(q, q_prev) * wa_ref[0:1, :] + q * wa_ref[1:2, :]
            + _shift_up(q, q_next) * wa_ref[2:3, :])
    ya_ref[...] = (p_ref[:, C:2 * C] * conv).astype(ya_ref.dtype)

    def hconv(j):
        lo, hi = H_OFF + j * D_HYENA, H_OFF + (j + 1) * D_HYENA
        wlo, whi = j * D_HYENA, (j + 1) * D_HYENA
        x = p_ref[:, lo:hi]
        xp = pp_ref[SUBLANES - 1:SUBLANES, lo:hi] * keep_prev
        xn = pn_ref[0:1, lo:hi] * keep_next
        return (_shift_down(x, xp) * wh_ref[0:1, wlo:whi] + x * wh_ref[1:2, wlo:whi]
                + _shift_up(x, xn) * wh_ref[2:3, wlo:whi])

    x0_ref[...] = hconv(0)
    u = hconv(2) * hconv(1)
    u_ref[...] = u
    ub_ref[...] = u.astype(ub_ref.dtype)


def _mix_front(proj, conv_a_w, conv_h_w, L, tl=256):
    T = proj.shape[0]
    W = G_OFF
    nb = tl // SUBLANES
    last_blk = T // SUBLANES - 1
    return pl.pallas_call(
        functools.partial(_mix_body, L // tl),
        grid=(T // tl,),
        in_specs=[
            pl.BlockSpec((tl, W), lambda i: (i, 0)),
            pl.BlockSpec((SUBLANES, W), lambda i: (jnp.maximum(i * nb - 1, 0), 0)),
            pl.BlockSpec((SUBLANES, W), lambda i: (jnp.minimum((i + 1) * nb, last_blk), 0)),
            pl.BlockSpec((SHORT_K, D_CONV), lambda i: (0, 0)),
            pl.BlockSpec((SHORT_K, 3 * D_HYENA), lambda i: (0, 0)),
        ],
        out_specs=[
            pl.BlockSpec((tl, D_CONV), lambda i: (i, 0)),
            pl.BlockSpec((tl, D_HYENA), lambda i: (i, 0)),
            pl.BlockSpec((tl, D_HYENA), lambda i: (i, 0)),
            pl.BlockSpec((tl, D_HYENA), lambda i: (i, 0)),
        ],
        out_shape=[
            jax.ShapeDtypeStruct((T, D_CONV), jnp.bfloat16),
            jax.ShapeDtypeStruct((T, D_HYENA), jnp.float32),
            jax.ShapeDtypeStruct((T, D_HYENA), jnp.float32),
            jax.ShapeDtypeStruct((T, D_HYENA), jnp.bfloat16),
        ],
        compiler_params=_params("parallel"),
        name="mix_front",
    )(proj, proj, proj, conv_a_w, conv_h_w)


@functools.lru_cache(maxsize=None)
def _dft_tables(L):
    m = (np.arange(L, dtype=np.int64)[:, None] * np.arange(L, dtype=np.int64)[None, :]) % (2 * L)
    ang = m.astype(np.float64) * (math.pi / L)
    c = np.cos(ang).astype(np.float32)
    s = np.sin(ang).astype(np.float32)
    alt = np.zeros((SUBLANES, L), np.float32)
    alt[0] = 1.0 - 2.0 * (np.arange(L) % 2)
    return c.astype(jnp.bfloat16), s.astype(jnp.bfloat16), alt.astype(jnp.bfloat16)


@functools.lru_cache(maxsize=None)
def _filter_consts(L):
    t = np.linspace(0.0, 1.0, L, dtype=np.float32)[:, None]
    bands = (FILTER_EMB - 1) // 2
    ang = (np.float32(2.0 * math.pi / L) * np.arange(L, dtype=np.float32)[:, None]
           * np.linspace(1e-4, bands - 1, bands, dtype=np.float32)[None, :])
    z = np.zeros((L, LANES), np.float32)
    z[:, 0:1] = t
    z[:, 1:1 + bands] = np.cos(ang)
    z[:, 1 + bands:1 + 2 * bands] = -np.sin(ang)
    max_decay = math.log(DECAY_TARGET) / DECAY_FAST
    min_decay = math.log(DECAY_TARGET) / DECAY_SLOW
    deltas = np.abs(np.linspace(min_decay, max_decay, D_HYENA, dtype=np.float32))[None, :]
    return z, deltas


def _filter_body(z_ref, dl_ref, w1_ref, b1_ref, w2_ref, b2_ref, w3_ref, b3_ref, wo_ref, fr_ref,
                 gs_hi_ref, gs_lo_ref, gd_hi_ref, gd_lo_ref):
    z = z_ref[...]
    fr = fr_ref[...]
    h = jnp.sin(fr * (jnp.dot(z, w1_ref[...], precision=_HP, preferred_element_type=jnp.float32) + b1_ref[...]))
    h = jnp.sin(fr * (jnp.dot(h, w2_ref[...], precision=_HP, preferred_element_type=jnp.float32) + b2_ref[...]))
    h = jnp.sin(fr * (jnp.dot(h, w3_ref[...], precision=_HP, preferred_element_type=jnp.float32) + b3_ref[...]))
    h = jnp.dot(h, wo_ref[...], precision=_HP, preferred_element_type=jnp.float32)
    decay = jnp.exp(-z[:, 0:1] * dl_ref[...])
    hf = h[:, :D_HYENA] * decay
    hb = h[:, D_HYENA:] * decay
    rows = lax.broadcasted_iota(jnp.int32, hb.shape, 0) + pl.program_id(0) * hb.shape[0]
    hb = jnp.where(rows == 0, 0.0, hb)
    gs = hf + hb
    gd = hb - hf
    gs_hi = gs.astype(jnp.bfloat16)
    gd_hi = gd.astype(jnp.bfloat16)
    gs_hi_ref[...] = gs_hi
    gd_hi_ref[...] = gd_hi
    gs_lo_ref[...] = (gs - gs_hi.astype(jnp.float32)).astype(jnp.bfloat16)
    gd_lo_ref[...] = (gd - gd_hi.astype(jnp.float32)).astype(jnp.bfloat16)


def _pad_to(a, shape):
    return jnp.pad(a, [(0, n - s) for s, n in zip(a.shape, shape)])


def _hyena_filter_taps(L, p, tl=512):
    z, deltas = _filter_consts(L)
    P = LANES
    w1 = _pad_to(p["filt_w1"], (P, P))
    w2 = _pad_to(p["filt_w2"], (P, P))
    w3 = _pad_to(p["filt_w3"], (P, P))
    wo = _pad_to(p["filt_w_out"], (P, 2 * D_HYENA))
    b1 = _pad_to(p["filt_b1"].reshape(1, -1), (1, P))
    b2 = _pad_to(p["filt_b2"].reshape(1, -1), (1, P))
    b3 = _pad_to(p["filt_b3"].reshape(1, -1), (1, P))
    fr = _pad_to(p["filt_freq"].reshape(1, -1), (1, P))
    const = lambda i: (0, 0)
    out = jax.ShapeDtypeStruct((L, D_HYENA), jnp.bfloat16)
    return pl.pallas_call(
        _filter_body,
        grid=(L // tl,),
        in_specs=[
            pl.BlockSpec((tl, P), lambda i: (i, 0)),
            pl.BlockSpec((1, D_HYENA), const),
            pl.BlockSpec((P, P), const), pl.BlockSpec((1, P), const),
            pl.BlockSpec((P, P), const), pl.BlockSpec((1, P), const),
            pl.BlockSpec((P, P), const), pl.BlockSpec((1, P), const),
            pl.BlockSpec((P, 2 * D_HYENA), const), pl.BlockSpec((1, P), const),
        ],
        out_specs=[pl.BlockSpec((tl, D_HYENA), lambda i: (i, 0))] * 4,
        out_shape=[out] * 4,
        compiler_params=_params("parallel"),
        name="hyena_filter",
    )(z, deltas, w1, b1, w2, b2, w3, b3, wo, fr)


def _spectrum_body(L, c_ref, s_ref, alt_ref, gsh_ref, gsl_ref, gdh_ref, gdl_ref,
                   p_ref, q_ref, kn_ref):
    c = c_ref[...]
    s = s_ref[...]
    tf = c.shape[0]
    f = lax.broadcasted_iota(jnp.int32, (tf, 1), 0) + pl.program_id(1) * tf
    scale = jnp.where(f == 0, 1.0, 2.0) * (1.0 / (2 * L))
    p_ref[...] = (_bdot(c, gsh_ref[...]) + _bdot(c, gsl_ref[...])) * scale
    q_ref[...] = (_bdot(s, gdh_ref[...]) + _bdot(s, gdl_ref[...])) * scale
    alt = alt_ref[...]
    kn_ref[...] = (_bdot(alt, gsh_ref[...]) + _bdot(alt, gsl_ref[...])) * (1.0 / (2 * L))


def _filter_spectrum(L, taps, tf=512, tc=256):
    cmat, smat, alt = _dft_tables(L)
    gcol = pl.BlockSpec((L, tc), lambda j, k: (0, j))
    return pl.pallas_call(
        functools.partial(_spectrum_body, L),
        grid=(D_HYENA // tc, L // tf),
        in_specs=[
            pl.BlockSpec((tf, L), lambda j, k: (k, 0)),
            pl.BlockSpec((tf, L), lambda j, k: (k, 0)),
            pl.BlockSpec((SUBLANES, L), lambda j, k: (0, 0)),
            gcol, gcol, gcol, gcol,
        ],
        out_specs=[
            pl.BlockSpec((tf, tc), lambda j, k: (k, j)),
            pl.BlockSpec((tf, tc), lambda j, k: (k, j)),
            pl.BlockSpec((SUBLANES, tc), lambda j, k: (0, j)),
        ],
        out_shape=[
            jax.ShapeDtypeStruct((L, D_HYENA), jnp.float32),
            jax.ShapeDtypeStruct((L, D_HYENA), jnp.float32),
            jax.ShapeDtypeStruct((SUBLANES, D_HYENA), jnp.float32),
        ],
        compiler_params=_params("parallel", "arbitrary"),
        name="filter_spectrum",
    )(cmat, smat, alt, *taps)


def _fwd_dft_body(c_ref, s_ref, alt_ref, u_ref, p_ref, q_ref, kn_ref, yr_ref, z2_ref, yn_ref):
    u = u_ref[0]
    a = _bdot(c_ref[...], u)
    b = _bdot(s_ref[...], u)
    p = p_ref[...]
    q = q_ref[...]
    yr_ref[0] = (a * p + b * q).astype(yr_ref.dtype)
    z2_ref[0] = (b * p - a * q).astype(z2_ref.dtype)
    yn_ref[0] = _bdot(alt_ref[...], u) * kn_ref[...]


def _fwd_dft(u_bf16, P, Q, KN, tf, tc):
    B, L, C = u_bf16.shape
    cmat, smat, alt = _dft_tables(L)
    return pl.pallas_call(
        _fwd_dft_body,
        grid=(B, C // tc, L // tf),
        in_specs=[
            pl.BlockSpec((tf, L), lambda b, j, k: (k, 0)),
            pl.BlockSpec((tf, L), lambda b, j, k: (k, 0)),
            pl.BlockSpec((SUBLANES, L), lambda b, j, k: (0, 0)),
            pl.BlockSpec((1, L, tc), lambda b, j, k: (b, 0, j)),
            pl.BlockSpec((tf, tc), lambda b, j, k: (k, j)),
            pl.BlockSpec((tf, tc), lambda b, j, k: (k, j)),
            pl.BlockSpec((SUBLANES, tc), lambda b, j, k: (0, j)),
        ],
        out_specs=[
            pl.BlockSpec((1, tf, tc), lambda b, j, k: (b, k, j)),
            pl.BlockSpec((1, tf, tc), lambda b, j, k: (b, k, j)),
            pl.BlockSpec((1, SUBLANES, tc), lambda b, j, k: (b, 0, j)),
        ],
        out_shape=[
            jax.ShapeDtypeStruct((B, L, C), jnp.bfloat16),
            jax.ShapeDtypeStruct((B, L, C), jnp.bfloat16),
            jax.ShapeDtypeStruct((B, SUBLANES, C), jnp.float32),
        ],
        compiler_params=_params("parallel", "parallel", "arbitrary"),
        name="fwd_dft",
    )(cmat, smat, alt, u_bf16, P, Q, KN)


def _inv_dft_body(c_ref, s_ref, yr_ref, z2_ref, yn_ref, x0_ref, u_ref, skip_ref, o_ref):
    y = _bdot(c_ref[...], yr_ref[0]) + _bdot(s_ref[...], z2_ref[0])
    tt = y.shape[0]
    t = lax.broadcasted_iota(jnp.int32, (tt, 1), 0) + pl.program_id(2) * tt
    sign = (1 - 2 * (t & 1)).astype(jnp.float32)
    y = y + sign * yn_ref[0, 0:1, :]
    o_ref[0] = (x0_ref[0] * (y + u_ref[0] * skip_ref[...])).astype(o_ref.dtype)


def _inv_dft(yr, z2, yn, x0, u, skip, tt, tc):
    B, L, C = yr.shape
    cmat, smat, _ = _dft_tables(L)
    return pl.pallas_call(
        _inv_dft_body,
        grid=(B, C // tc, L // tt),
        in_specs=[
            pl.BlockSpec((tt, L), lambda b, j, k: (k, 0)),
            pl.BlockSpec((tt, L), lambda b, j, k: (k, 0)),
            pl.BlockSpec((1, L, tc), lambda b, j, k: (b, 0, j)),
            pl.BlockSpec((1, L, tc), lambda b, j, k: (b, 0, j)),
            pl.BlockSpec((1, SUBLANES, tc), lambda b, j, k: (b, 0, j)),
            pl.BlockSpec((1, tt, tc), lambda b, j, k: (b, k, j)),
            pl.BlockSpec((1, tt, tc), lambda b, j, k: (b, k, j)),
            pl.BlockSpec((1, tc), lambda b, j, k: (0, j)),
        ],
        out_specs=pl.BlockSpec((1, tt, tc), lambda b, j, k: (b, k, j)),
        out_shape=jax.ShapeDtypeStruct((B, L, C), jnp.bfloat16),
        compiler_params=_params("parallel", "parallel", "arbitrary"),
        name="inv_dft",
    )(cmat, smat, yr, z2, yn, x0, u, skip.reshape(1, C))


def _merge_body(ya_ref, yb_ref, ga_ref, gb_ref, x_ref, wa_ref, wh_ref, wo_ref, gn_ref, wr_ref,
                h_ref, xn_ref, lg_ref):
    pa = _bdot(ya_ref[...], wa_ref[...])
    ph = _bdot(yb_ref[...], wh_ref[...])
    merged = jax.nn.sigmoid(ga_ref[...]) * pa + jax.nn.sigmoid(gb_ref[...]) * ph
    h = x_ref[...] + _bdot(merged.astype(jnp.bfloat16), wo_ref[...])
    h_ref[...] = h
    xn = _rms(h, gn_ref[...])
    xn_ref[...] = xn
    lg_ref[...] = jnp.dot(xn, wr_ref[...], preferred_element_type=jnp.float32, precision=_HP)


def _merge_out(ya, yb, proj, x, wa, wh, wo, g_ffn, w_router, tm=256):
    T, D = x.shape
    const = lambda i: (0, 0)
    row = lambda i: (i, 0)
    one = pl.Buffered(1)
    ga_blk = G_OFF // D
    return pl.pallas_call(
        _merge_body,
        grid=(T // tm,),
        in_specs=[
            pl.BlockSpec((tm, D_CONV), row),
            pl.BlockSpec((tm, D_HYENA), row),
            pl.BlockSpec((tm, D), lambda i: (i, ga_blk)),
            pl.BlockSpec((tm, D), lambda i: (i, ga_blk + 1)),
            pl.BlockSpec((tm, D), row),
            pl.BlockSpec((D_CONV, D), const, pipeline_mode=one),
            pl.BlockSpec((D_HYENA, D), const, pipeline_mode=one),
            pl.BlockSpec((D, D), const, pipeline_mode=one),
            pl.BlockSpec((1, D), const, pipeline_mode=one),
            pl.BlockSpec((D, N_EXPERTS), const, pipeline_mode=one),
        ],
        out_specs=[
            pl.BlockSpec((tm, D), row),
            pl.BlockSpec((tm, D), row),
            pl.BlockSpec((tm, N_EXPERTS), row),
        ],
        out_shape=[
            jax.ShapeDtypeStruct((T, D), jnp.float32),
            jax.ShapeDtypeStruct((T, D), jnp.float32),
            jax.ShapeDtypeStruct((T, N_EXPERTS), jnp.float32),
        ],
        compiler_params=_params("parallel"),
        name="merge_out",
    )(ya, yb, proj, proj, x, wa, wh, wo, g_ffn.reshape(1, D), w_router)


def _expert_body(x_ref, g_ref, wg_ref, wu_ref, wd_ref, o_ref):
    f = pl.program_id(2)
    x = x_ref[0]
    a = _bdot(x, wg_ref[0].astype(jnp.bfloat16))
    b = _bdot(x, wu_ref[0].astype(jnp.bfloat16))
    hmid = (a * jax.nn.sigmoid(a) * b).astype(jnp.bfloat16)
    part = _bdot(hmid, wd_ref[0].astype(jnp.bfloat16))

    @pl.when(f == 0)
    def _():
        o_ref[0] = part

    @pl.when(f > 0)
    def _():
        o_ref[0] += part

    @pl.when(f == pl.num_programs(2) - 1)
    def _():
        o_ref[0] = o_ref[0] * g_ref[0]


def _expert_ffn(xe_bf16, g, w_gate, w_up, w_down, tf=256):
    E, C, D = xe_bf16.shape
    F = w_gate.shape[-1]
    tm = min(C, 1024)
    return pl.pallas_call(
        _expert_body,
        grid=(E, C // tm, F // tf),
        in_specs=[
            pl.BlockSpec((1, tm, D), lambda e, i, f: (e, i, 0)),
            pl.BlockSpec((1, tm, 1), lambda e, i, f: (e, i, 0)),
            pl.BlockSpec((1, D, tf), lambda e, i, f: (e, 0, f)),
            pl.BlockSpec((1, D, tf), lambda e, i, f: (e, 0, f)),
            pl.BlockSpec((1, tf, D), lambda e, i, f: (e, f, 0)),
        ],
        out_specs=pl.BlockSpec((1, tm, D), lambda e, i, f: (e, i, 0)),
        out_shape=jax.ShapeDtypeStruct((E, C, D), jnp.float32),
        compiler_params=_params("parallel", "parallel", "arbitrary"),
        name="expert_ffn",
    )(xe_bf16, g.reshape(E, C, 1), w_gate, w_up, w_down)


def _final_norm_body(x_ref, g_ref, o_ref):
    o_ref[...] = _rms(x_ref[...], g_ref[...])


def _final_norm(x, g, tm=512):
    T, D = x.shape
    return pl.pallas_call(
        _final_norm_body,
        grid=(T // tm,),
        in_specs=[pl.BlockSpec((tm, D), lambda i: (i, 0)),
                  pl.BlockSpec((1, D), lambda i: (0, 0))],
        out_specs=pl.BlockSpec((tm, D), lambda i: (i, 0)),
        out_shape=jax.ShapeDtypeStruct((T, D), jnp.float32),
        compiler_params=_params("parallel"),
        name="final_norm",
    )(x, g.reshape(1, D))


def _trunk(x, p):
    B, L, D = x.shape
    T = B * L
    xf = x.reshape(T, D)
    proj = _norm_proj(xf, p["norm_mix"], p["w_in"])
    ya, x0, u, ub = _mix_front(proj, p["conv_a_w"], p["conv_h_w"], L)

    taps = _hyena_filter_taps(L, p)
    P, Q, KN = _filter_spectrum(L, taps)
    tc = 1024 if L <= 2048 else 512
    shp = (B, L, D_HYENA)
    yr, z2, yn = _fwd_dft(ub.reshape(shp), P, Q, KN, tf=512, tc=tc)
    yb = _inv_dft(yr, z2, yn, x0.reshape(shp), u.reshape(shp), p["hyena_skip"], tt=512, tc=tc)

    h, xn, logits = _merge_out(ya, yb.reshape(T, D_HYENA), proj, xf, p["w_proj_a"], p["w_proj_h"],
                               p["w_out"], p["norm_ffn"], p["w_router"])

    cap = max(1, CAPACITY_FACTOR * T // N_EXPERTS)
    aff = jax.nn.softmax(logits, axis=-1)
    g, idx = lax.top_k(aff.T, cap)
    xe = xn.astype(jnp.bfloat16)[idx]
    ye = _expert_ffn(xe, g, p["w_gate"], p["w_up"], p["w_down"])
    h2 = h.at[idx.reshape(-1)].add(ye.reshape(-1, D))
    return _final_norm(h2, p["norm_final"]).reshape(B, L, D)


def kernel(x_prompt, x_sample, w_in, conv_a_w, conv_h_w, filt_w1, filt_b1, filt_w2, filt_b2, filt_w3, filt_b3, filt_w_out, filt_freq, hyena_skip, w_proj_a, w_proj_h, w_out, norm_mix, norm_ffn, w_router, w_gate, w_up, w_down, norm_final):
    bf16 = jnp.bfloat16
    p = dict(
        w_in=w_in[0].astype(bf16), conv_a_w=conv_a_w[0], conv_h_w=conv_h_w[0],
        filt_w1=filt_w1[0], filt_b1=filt_b1[0], filt_w2=filt_w2[0], filt_b2=filt_b2[0],
        filt_w3=filt_w3[0], filt_b3=filt_b3[0], filt_w_out=filt_w_out[0], filt_freq=filt_freq[0],
        hyena_skip=hyena_skip[0], w_proj_a=w_proj_a[0].astype(bf16), w_proj_h=w_proj_h[0].astype(bf16),
        w_out=w_out[0].astype(bf16), norm_mix=norm_mix[0], norm_ffn=norm_ffn[0],
        w_router=w_router[0], w_gate=w_gate[0], w_up=w_up[0], w_down=w_down[0],
        norm_final=norm_final,
    )
    return (_trunk(x_prompt, p), _trunk(x_sample, p))
```

```python
import functools
import math

import numpy as np

import jax
import jax.numpy as jnp
from jax import lax
from jax.experimental import pallas as pl
from jax.experimental.pallas import tpu as pltpu

D_MODEL = 2048
D_CONV = 1024
D_HYENA = 1024
SHORT_K = 3
FILTER_EMB = 33
FILTER_ORDER = 64
DECAY_FAST = 0.3
DECAY_SLOW = 1.5
DECAY_TARGET = 1e-2
N_EXPERTS = 16
CAPACITY_FACTOR = 2
D_EXPERT = 5632
EPS = 1e-6
D_IN = 3 * D_CONV + 3 * D_HYENA + 2 * D_MODEL
H_OFF = 3 * D_CONV
G_OFF = 3 * D_CONV + 3 * D_HYENA

LANES = 128
SUBLANES = 8
VMEM_LIMIT_BYTES = 56 * 1024 * 1024

_HP = lax.Precision.HIGHEST


def _rms(x, g):
    r = lax.rsqrt(jnp.mean(x * x, axis=-1, keepdims=True) + EPS)
    return (x * r) * g


def _bdot(a, b):
    return jnp.dot(a, b, preferred_element_type=jnp.float32)


def _params(*sem):
    return pltpu.CompilerParams(dimension_semantics=sem, vmem_limit_bytes=VMEM_LIMIT_BYTES)


def _norm_proj_body(x_ref, g_ref, w_ref, o_ref, xn_ref):
    @pl.when(pl.program_id(1) == 0)
    def _():
        xn_ref[...] = _rms(x_ref[...], g_ref[...]).astype(jnp.bfloat16)

    o_ref[...] = _bdot(xn_ref[...], w_ref[...])


def _norm_proj(x, g, w_bf16, tm=1024, tn=1024):
    T, D = x.shape
    N = w_bf16.shape[1]
    return pl.pallas_call(
        _norm_proj_body,
        grid=(T // tm, N // tn),
        in_specs=[
            pl.BlockSpec((tm, D), lambda i, j: (i, 0)),
            pl.BlockSpec((1, D), lambda i, j: (0, 0)),
            pl.BlockSpec((D, tn), lambda i, j: (0, j)),
        ],
        out_specs=pl.BlockSpec((tm, tn), lambda i, j: (i, j)),
        out_shape=jax.ShapeDtypeStruct((T, N), jnp.float32),
        scratch_shapes=[pltpu.VMEM((tm, D), jnp.bfloat16)],
        compiler_params=_params("parallel", "arbitrary"),
        name="norm_proj",
    )(x, g.reshape(1, D), w_bf16)


def _shift_down(x, first_row):
    rows = lax.broadcasted_iota(jnp.int32, x.shape, 0)
    return jnp.where(rows == 0, first_row, pltpu.roll(x, 1, axis=0))


def _shift_up(x, last_row):
    n = x.shape[0]
    rows = lax.broadcasted_iota(jnp.int32, x.shape, 0)
    return jnp.where(rows == n - 1, last_row, pltpu.roll(x, n - 1, axis=0))


def _mix_body(tiles_per_seq, p_ref, pp_ref, pn_ref, wa_ref, wh_ref,
              ya_ref, x0_ref, u_ref, ub_ref):
    i = pl.program_id(0)
    keep_prev = jnp.where(i % tiles_per_seq == 0, 0.0, 1.0)
    keep_next = jnp.where(i % tiles_per_seq == tiles_per_seq - 1, 0.0, 1.0)
    C = D_CONV
    q = p_ref[:, 0:C] * p_ref[:, 2 * C:3 * C]
    q_prev = pp_ref[SUBLANES - 1:SUBLANES, 0:C] * pp_ref[SUBLANES - 1:SUBLANES, 2 * C:3 * C] * keep_prev
    q_next = pn_ref[0:1, 0:C] * pn_ref[0:1, 2 * C:3 * C] * keep_next
    conv = (_shift_down(q, q_prev) * wa_ref[0:1, :] + q * wa_ref[1:2, :]
            + _shift_up(q, q_next) * wa_ref[2:3, :])
    ya_ref[...] = (p_ref[:, C:2 * C] * conv).astype(ya_ref.dtype)

    def hconv(j):
        lo, hi = H_OFF + j * D_HYENA, H_OFF + (j + 1) * D_HYENA
        wlo, whi = j * D_HYENA, (j + 1) * D_HYENA
        x = p_ref[:, lo:hi]
        xp = pp_ref[SUBLANES - 1:SUBLANES, lo:hi] * keep_prev
        xn = pn_ref[0:1, lo:hi] * keep_next
        return (_shift_down(x, xp) * wh_ref[0:1, wlo:whi] + x * wh_ref[1:2, wlo:whi]
                + _shift_up(x, xn) * wh_ref[2:3, wlo:whi])

    x0_ref[...] = hconv(0)
    u = hconv(2) * hconv(1)
    u_ref[...] = u
    ub_ref[...] = u.astype(ub_ref.dtype)


def _mix_front(proj, conv_a_w, conv_h_w, L, tl=256):
    T = proj.shape[0]
    W = G_OFF
    nb = tl // SUBLANES
    last_blk = T // SUBLANES - 1
    return pl.pallas_call(
        functools.partial(_mix_body, L // tl),
        grid=(T // tl,),
        in_specs=[
            pl.BlockSpec((tl, W), lambda i: (i, 0)),
            pl.BlockSpec((SUBLANES, W), lambda i: (jnp.maximum(i * nb - 1, 0), 0)),
            pl.BlockSpec((SUBLANES, W), lambda i: (jnp.minimum((i + 1) * nb, last_blk), 0)),
            pl.BlockSpec((SHORT_K, D_CONV), lambda i: (0, 0)),
            pl.BlockSpec((SHORT_K, 3 * D_HYENA), lambda i: (0, 0)),
        ],
        out_specs=[
            pl.BlockSpec((tl, D_CONV), lambda i: (i, 0)),
            pl.BlockSpec((tl, D_HYENA), lambda i: (i, 0)),
            pl.BlockSpec((tl, D_HYENA), lambda i: (i, 0)),
            pl.BlockSpec((tl, D_HYENA), lambda i: (i, 0)),
        ],
        out_shape=[
            jax.ShapeDtypeStruct((T, D_CONV), jnp.bfloat16),
            jax.ShapeDtypeStruct((T, D_HYENA), jnp.float32),
            jax.ShapeDtypeStruct((T, D_HYENA), jnp.float32),
            jax.ShapeDtypeStruct((T, D_HYENA), jnp.bfloat16),
        ],
        compiler_params=_params("parallel"),
        name="mix_front",
    )(proj, proj, proj, conv_a_w, conv_h_w)


@functools.lru_cache(maxsize=None)
def _dft_tables(L):
    m = (np.arange(L, dtype=np.int64)[:, None] * np.arange(L, dtype=np.int64)[None, :]) % (2 * L)
    ang = m.astype(np.float64) * (math.pi / L)
    c = np.cos(ang).astype(np.float32)
    s = np.sin(ang).astype(np.float32)
    alt = np.zeros((SUBLANES, L), np.float32)
    alt[0] = 1.0 - 2.0 * (np.arange(L) % 2)
    return c.astype(jnp.bfloat16), s.astype(jnp.bfloat16), alt.astype(jnp.bfloat16)


@functools.lru_cache(maxsize=None)
def _filter_consts(L):
    t = np.linspace(0.0, 1.0, L, dtype=np.float32)[:, None]
    bands = (FILTER_EMB - 1) // 2
    ang = (np.float32(2.0 * math.pi / L) * np.arange(L, dtype=np.float32)[:, None]
           * np.linspace(1e-4, bands - 1, bands, dtype=np.float32)[None, :])
    z = np.zeros((L, LANES), np.float32)
    z[:, 0:1] = t
    z[:, 1:1 + bands] = np.cos(ang)
    z[:, 1 + bands:1 + 2 * bands] = -np.sin(ang)
    max_decay = math.log(DECAY_TARGET) / DECAY_FAST
    min_decay = math.log(DECAY_TARGET) / DECAY_SLOW
    deltas = np.abs(np.linspace(min_decay, max_decay, D_HYENA, dtype=np.float32))[None, :]
    return z, deltas


def _filter_body(z_ref, dl_ref, w1_ref, b1_ref, w2_ref, b2_ref, w3_ref, b3_ref, wo_ref, fr_ref,
                 gs_hi_ref, gs_lo_ref, gd_hi_ref, gd_lo_ref):
    z = z_ref[...]
    fr = fr_ref[...]
    h = jnp.sin(fr * (jnp.dot(z, w1_ref[...], precision=_HP, preferred_element_type=jnp.float32) + b1_ref[...]))
    h = jnp.sin(fr * (jnp.dot(h, w2_ref[...], precision=_HP, preferred_element_type=jnp.float32) + b2_ref[...]))
    h = jnp.sin(fr * (jnp.dot(h, w3_ref[...], precision=_HP, preferred_element_type=jnp.float32) + b3_ref[...]))
    h = jnp.dot(h, wo_ref[...], precision=_HP, preferred_element_type=jnp.float32)
    decay = jnp.exp(-z[:, 0:1] * dl_ref[...])
    hf = h[:, :D_HYENA] * decay
    hb = h[:, D_HYENA:] * decay
    rows = lax.broadcasted_iota(jnp.int32, hb.shape, 0) + pl.program_id(0) * hb.shape[0]
    hb = jnp.where(rows == 0, 0.0, hb)
    gs = hf + hb
    gd = hb - hf
    gs_hi = gs.astype(jnp.bfloat16)
    gd_hi = gd.astype(jnp.bfloat16)
    gs_hi_ref[...] = gs_hi
    gd_hi_ref[...] = gd_hi
    gs_lo_ref[...] = (gs - gs_hi.astype(jnp.float32)).astype(jnp.bfloat16)
    gd_lo_ref[...] = (gd - gd_hi.astype(jnp.float32)).astype(jnp.bfloat16)


def _pad_to(a, shape):
    return jnp.pad(a, [(0, n - s) for s, n in zip(a.shape, shape)])


def _hyena_filter_taps(L, p, tl=512):
    z, deltas = _filter_consts(L)
    P = LANES
    w1 = _pad_to(p["filt_w1"], (P, P))
    w2 = _pad_to(p["filt_w2"], (P, P))
    w3 = _pad_to(p["filt_w3"], (P, P))
    wo = _pad_to(p["filt_w_out"], (P, 2 * D_HYENA))
    b1 = _pad_to(p["filt_b1"].reshape(1, -1), (1, P))
    b2 = _pad_to(p["filt_b2"].reshape(1, -1), (1, P))
    b3 = _pad_to(p["filt_b3"].reshape(1, -1), (1, P))
    fr = _pad_to(p["filt_freq"].reshape(1, -1), (1, P))
    const = lambda i: (0, 0)
    out = jax.ShapeDtypeStruct((L, D_HYENA), jnp.bfloat16)
    return pl.pallas_call(
        _filter_body,
        grid=(L // tl,),
        in_specs=[
            pl.BlockSpec((tl, P), lambda i: (i, 0)),
            pl.BlockSpec((1, D_HYENA), const),
            pl.BlockSpec((P, P), const), pl.BlockSpec((1, P), const),
            pl.BlockSpec((P, P), const), pl.BlockSpec((1, P), const),
            pl.BlockSpec((P, P), const), pl.BlockSpec((1, P), const),
            pl.BlockSpec((P, 2 * D_HYENA), const), pl.BlockSpec((1, P), const),
        ],
        out_specs=[pl.BlockSpec((tl, D_HYENA), lambda i: (i, 0))] * 4,
        out_shape=[out] * 4,
        compiler_params=_params("parallel"),
        name="hyena_filter",
    )(z, deltas, w1, b1, w2, b2, w3, b3, wo, fr)


def _spectrum_body(L, c_ref, s_ref, alt_ref, gsh_ref, gsl_ref, gdh_ref, gdl_ref,
                   p_ref, q_ref, kn_ref):
    c = c_ref[...]
    s = s_ref[...]
    tf = c.shape[0]
    f = lax.broadcasted_iota(jnp.int32, (tf, 1), 0) + pl.program_id(1) * tf
    scale = jnp.where(f == 0, 1.0, 2.0) * (1.0 / (2 * L))
    p_ref[...] = (_bdot(c, gsh_ref[...]) + _bdot(c, gsl_ref[...])) * scale
    q_ref[...] = (_bdot(s, gdh_ref[...]) + _bdot(s, gdl_ref[...])) * scale
    alt = alt_ref[...]
    kn_ref[...] = (_bdot(alt, gsh_ref[...]) + _bdot(alt, gsl_ref[...])) * (1.0 / (2 * L))


def _filter_spectrum(L, taps, tf=512, tc=256):
    cmat, smat, alt = _dft_tables(L)
    gcol = pl.BlockSpec((L, tc), lambda j, k: (0, j))
    return pl.pallas_call(
        functools.partial(_spectrum_body, L),
        grid=(D_HYENA // tc, L // tf),
        in_specs=[
            pl.BlockSpec((tf, L), lambda j, k: (k, 0)),
            pl.BlockSpec((tf, L), lambda j, k: (k, 0)),
            pl.BlockSpec((SUBLANES, L), lambda j, k: (0, 0)),
            gcol, gcol, gcol, gcol,
        ],
        out_specs=[
            pl.BlockSpec((tf, tc), lambda j, k: (k, j)),
            pl.BlockSpec((tf, tc), lambda j, k: (k, j)),
            pl.BlockSpec((SUBLANES, tc), lambda j, k: (0, j)),
        ],
        out_shape=[
            jax.ShapeDtypeStruct((L, D_HYENA), jnp.float32),
            jax.ShapeDtypeStruct((L, D_HYENA), jnp.float32),
            jax.ShapeDtypeStruct((SUBLANES, D_HYENA), jnp.float32),
        ],
        compiler_params=_params("parallel", "arbitrary"),
        name="filter_spectrum",
    )(cmat, smat, alt, *taps)


def _fwd_dft_body(c_ref, s_ref, alt_ref, u_ref, p_ref, q_ref, kn_ref, yr_ref, z2_ref, yn_ref):
    u = u_ref[0]
    a = _bdot(c_ref[...], u)
    b = _bdot(s_ref[...], u)
    p = p_ref[...]
    q = q_ref[...]
    yr_ref[0] = (a * p + b * q).astype(yr_ref.dtype)
    z2_ref[0] = (b * p - a * q).astype(z2_ref.dtype)
    yn_ref[0] = _bdot(alt_ref[...], u) * kn_ref[...]


def _fwd_dft(u_bf16, P, Q, KN, tf, tc):
    B, L, C = u_bf16.shape
    cmat, smat, alt = _dft_tables(L)
    return pl.pallas_call(
        _fwd_dft_body,
        grid=(B, C // tc, L // tf),
        in_specs=[
            pl.BlockSpec((tf, L), lambda b, j, k: (k, 0)),
            pl.BlockSpec((tf, L), lambda b, j, k: (k, 0)),
            pl.BlockSpec((SUBLANES, L), lambda b, j, k: (0, 0)),
            pl.BlockSpec((1, L, tc), lambda b, j, k: (b, 0, j)),
            pl.BlockSpec((tf, tc), lambda b, j, k: (k, j)),
            pl.BlockSpec((tf, tc), lambda b, j, k: (k, j)),
            pl.BlockSpec((SUBLANES, tc), lambda b, j, k: (0, j)),
        ],
        out_specs=[
            pl.BlockSpec((1, tf, tc), lambda b, j, k: (b, k, j)),
            pl.BlockSpec((1, tf, tc), lambda b, j, k: (b, k, j)),
            pl.BlockSpec((1, SUBLANES, tc), lambda b, j, k: (b, 0, j)),
        ],
        out_shape=[
            jax.ShapeDtypeStruct((B, L, C), jnp.bfloat16),
            jax.ShapeDtypeStruct((B, L, C), jnp.bfloat16),
            jax.ShapeDtypeStruct((B, SUBLANES, C), jnp.float32),
        ],
        compiler_params=_params("parallel", "parallel", "arbitrary"),
        name="fwd_dft",
    )(cmat, smat, alt, u_bf16, P, Q, KN)


def _inv_dft_body(c_ref, s_ref, yr_ref, z2_ref, yn_ref, x0_ref, u_ref, skip_ref, o_ref):
    y = _bdot(c_ref[...], yr_ref[0]) + _bdot(s_ref[...], z2_ref[0])
    tt = y.shape[0]
    t = lax.broadcasted_iota(jnp.int32, (tt, 1), 0) + pl.program_id(2) * tt
    sign = (1 - 2 * (t & 1)).astype(jnp.float32)
    y = y + sign * yn_ref[0, 0:1, :]
    o_ref[0] = (x0_ref[0] * (y + u_ref[0] * skip_ref[...])).astype(o_ref.dtype)


def _inv_dft(yr, z2, yn, x0, u, skip, tt, tc):
    B, L, C = yr.shape
    cmat, smat, _ = _dft_tables(L)
    return pl.pallas_call(
        _inv_dft_body,
        grid=(B, C // tc, L // tt),
        in_specs=[
            pl.BlockSpec((tt, L), lambda b, j, k: (k, 0)),
            pl.BlockSpec((tt, L), lambda b, j, k: (k, 0)),
            pl.BlockSpec((1, L, tc), lambda b, j, k: (b, 0, j)),
            pl.BlockSpec((1, L, tc), lambda b, j, k: (b, 0, j)),
            pl.BlockSpec((1, SUBLANES, tc), lambda b, j, k: (b, 0, j)),
            pl.BlockSpec((1, tt, tc), lambda b, j, k: (b, k, j)),
            pl.BlockSpec((1, tt, tc), lambda b, j, k: (b, k, j)),
            pl.BlockSpec((1, tc), lambda b, j, k: (0, j)),
        ],
        out_specs=pl.BlockSpec((1, tt, tc), lambda b, j, k: (b, k, j)),
        out_shape=jax.ShapeDtypeStruct((B, L, C), jnp.bfloat16),
        compiler_params=_params("parallel", "parallel", "arbitrary"),
        name="inv_dft",
    )(cmat, smat, yr, z2, yn, x0, u, skip.reshape(1, C))


def _store_row_split(ref, x):
    rpt = x.shape[1] // LANES
    n = x.shape[0]
    for k in range(rpt):
        ref[pl.ds(k, n, stride=rpt), :] = x[:, k * LANES:(k + 1) * LANES]


def _load_row_split(ref, n, rpt):
    return jnp.concatenate([ref[pl.ds(k, n, stride=rpt), :] for k in range(rpt)], axis=1)


def _merge_body(ya_ref, yb_ref, ga_ref, gb_ref, x_ref, wa_ref, wh_ref, wo_ref, gn_ref, wrt_ref,
                h_ref, xn_ref, aff_ref):
    pa = _bdot(ya_ref[...], wa_ref[...])
    ph = _bdot(yb_ref[...], wh_ref[...])
    merged = jax.nn.sigmoid(ga_ref[...]) * pa + jax.nn.sigmoid(gb_ref[...]) * ph
    h = x_ref[...] + _bdot(merged.astype(jnp.bfloat16), wo_ref[...])
    _store_row_split(h_ref, h)
    xn = _rms(h, gn_ref[...])
    _store_row_split(xn_ref, xn)
    lg = lax.dot_general(wrt_ref[...], xn, (((1,), (1,)), ((), ())),
                         preferred_element_type=jnp.float32, precision=_HP)
    ex = jnp.exp(lg - jnp.max(lg, axis=0, keepdims=True))
    aff_ref[...] = ex / jnp.sum(ex, axis=0, keepdims=True)


def _merge_out(ya, yb, proj, x, wa, wh, wo, g_ffn, w_router_t, tm=256):
    T, D = x.shape
    rpt = D // LANES
    const = lambda i: (0, 0)
    row = lambda i: (i, 0)
    one = pl.Buffered(1)
    ga_blk = G_OFF // D
    return pl.pallas_call(
        _merge_body,
        grid=(T // tm,),
        in_specs=[
            pl.BlockSpec((tm, D_CONV), row),
            pl.BlockSpec((tm, D_HYENA), row),
            pl.BlockSpec((tm, D), lambda i: (i, ga_blk)),
            pl.BlockSpec((tm, D), lambda i: (i, ga_blk + 1)),
            pl.BlockSpec((tm, D), row),
            pl.BlockSpec((D_CONV, D), const, pipeline_mode=one),
            pl.BlockSpec((D_HYENA, D), const, pipeline_mode=one),
            pl.BlockSpec((D, D), const, pipeline_mode=one),
            pl.BlockSpec((1, D), const, pipeline_mode=one),
            pl.BlockSpec((N_EXPERTS, D), const, pipeline_mode=one),
        ],
        out_specs=[
            pl.BlockSpec((tm * rpt, LANES), row),
            pl.BlockSpec((tm * rpt, LANES), row),
            pl.BlockSpec((N_EXPERTS, tm), lambda i: (0, i)),
        ],
        out_shape=[
            jax.ShapeDtypeStruct((T * rpt, LANES), jnp.float32),
            jax.ShapeDtypeStruct((T * rpt, LANES), jnp.float32),
            jax.ShapeDtypeStruct((N_EXPERTS, T), jnp.float32),
        ],
        compiler_params=_params("parallel"),
        name="merge_out",
    )(ya, yb, proj, proj, x, wa, wh, wo, g_ffn.reshape(1, D), w_router_t)


def _route_body(cap, aff_ref, idx_ref, g_ref):
    f32, bf16 = jnp.float32, jnp.bfloat16
    x = aff_ref[0]
    R = x.shape[0]
    bits = pltpu.bitcast(x, jnp.int32)

    def count(m):
        return jnp.sum(jnp.where(m, 1.0, 0.0), keepdims=True)

    def bit_step(i, cur):
        cand = cur | jnp.left_shift(jnp.int32(1), 30 - i)
        return jnp.where(count(bits >= cand) >= cap, cand, cur)

    thr = lax.fori_loop(0, 31, bit_step, jnp.zeros((1, 1), jnp.int32))

    def tri(n, cmp):
        a = lax.broadcasted_iota(jnp.int32, (n, n), 0)
        b = lax.broadcasted_iota(jnp.int32, (n, n), 1)
        return jnp.where(cmp(a, b), 1.0, 0.0).astype(bf16)

    upper = tri(LANES, lambda k, l: k <= l)
    ones = jnp.ones((LANES, LANES), bf16)
    strict_lower = tri(R, lambda r, k: k < r)

    def prefix(m01):
        mb = m01.astype(bf16)
        rowcum = _bdot(mb, upper)
        rowtot = _bdot(mb, ones)
        before = _bdot(strict_lower, rowtot.astype(bf16))
        return mb, rowcum, rowtot, before

    gt = jnp.where(bits > thr, 1.0, 0.0)
    eq = jnp.where(bits == thr, 1.0, 0.0)
    _, eq_cum, _, eq_before = prefix(eq)
    need = cap - jnp.sum(gt, keepdims=True)
    sel = gt + eq * jnp.where(eq_cum + eq_before <= need, 1.0, 0.0)
    selb, rowcum, rowtot, before = prefix(sel)

    through_row = (before + rowtot)[:, 0:1]
    j = lax.broadcasted_iota(jnp.int32, (1, cap), 1).astype(f32)
    done = through_row <= j
    r_j = jnp.sum(jnp.where(done, 1.0, 0.0), axis=0, keepdims=True)
    before_j = jnp.sum(jnp.where(done, rowtot[:, 0:1], 0.0), axis=0, keepdims=True)
    rows = lax.broadcasted_iota(jnp.int32, (R, cap), 0).astype(f32)
    pick_row = jnp.where(rows == r_j, 1.0, 0.0).astype(bf16)

    nt = (((1,), (1,)), ((), ()))
    lower = tri(LANES, lambda l, k: k <= l)
    rowcum_t = lax.dot_general(lower, selb, nt, preferred_element_type=f32)
    cum_j = _bdot(rowcum_t.astype(bf16), pick_row)
    lane_j = jnp.sum(jnp.where(cum_j <= j - before_j, 1.0, 0.0), axis=0, keepdims=True)
    idx_ref[0] = (r_j * LANES + lane_j).astype(jnp.int32)

    eye = tri(LANES, lambda a, b: a == b)
    hi = x.astype(bf16)
    r1 = x - hi.astype(f32)
    mid = r1.astype(bf16)
    lo = (r1 - mid.astype(f32)).astype(bf16)

    def pick(part):
        part_t = lax.dot_general(eye, part, nt, preferred_element_type=f32)
        return _bdot(part_t.astype(bf16), pick_row)

    aff_j = (pick(hi) + pick(mid)) + pick(lo)
    lanes = lax.broadcasted_iota(jnp.int32, (LANES, cap), 0).astype(f32)
    g_ref[0] = jnp.sum(jnp.where(lanes == lane_j, aff_j, 0.0), axis=0, keepdims=True)


def _route(aff_t, cap):
    E, T = aff_t.shape
    R = T // LANES
    return pl.pallas_call(
        functools.partial(_route_body, cap),
        grid=(E,),
        in_specs=[pl.BlockSpec((1, R, LANES), lambda e: (e, 0, 0))],
        out_specs=[pl.BlockSpec((1, 1, cap), lambda e: (e, 0, 0)),
                   pl.BlockSpec((1, 1, cap), lambda e: (e, 0, 0))],
        out_shape=[jax.ShapeDtypeStruct((E, 1, cap), jnp.int32),
                   jax.ShapeDtypeStruct((E, 1, cap), jnp.float32)],
        compiler_params=_params("parallel"),
        name="route",
    )(aff_t.reshape(E, R, LANES))


_SEM_X, _SEM_ACC, _SEM_OUT, _SEM_IDX = range(4)


def _moe_body(tm, rpt, idx_hbm, g_ref, wg_ref, wu_ref, wd_ref, xn_hbm, h_hbm, out_hbm,
              idx_smem, rows, xb, acc, sem):
    del h_hbm
    e, i, f = pl.program_id(0), pl.program_id(1), pl.program_id(2)
    last = pl.num_programs(2) - 1

    def slab(ref, r):
        return ref.at[pl.ds(pl.multiple_of(r * rpt, rpt), rpt), :]

    def start_rows(make):
        def body(j, c):
            make(j, idx_smem[j]).start()
            return c
        lax.fori_loop(0, tm, body, 0)

    def whole(src, dst, s):
        return pltpu.make_async_copy(src, dst, sem.at[s])

    all_rows = pl.ds(0, tm * rpt)

    @pl.when(f == 0)
    def _():
        cp = pltpu.make_async_copy(idx_hbm.at[e, i], idx_smem, sem.at[_SEM_IDX])
        cp.start()
        cp.wait()
        start_rows(lambda j, t: pltpu.make_async_copy(slab(xn_hbm, t), slab(rows, j), sem.at[_SEM_X]))
        whole(xn_hbm.at[all_rows, :], rows, _SEM_X).wait()
        xb[...] = _load_row_split(rows, tm, rpt).astype(jnp.bfloat16)
        start_rows(lambda j, t: pltpu.make_async_copy(slab(out_hbm, t), slab(rows, j), sem.at[_SEM_ACC]))
        acc[...] = jnp.zeros_like(acc)

    x = xb[...]
    a = _bdot(x, wg_ref[0].astype(jnp.bfloat16))
    b = _bdot(x, wu_ref[0].astype(jnp.bfloat16))
    hmid = (a * jax.nn.sigmoid(a) * b).astype(jnp.bfloat16)
    acc[...] += _bdot(hmid, wd_ref[0].astype(jnp.bfloat16))

    @pl.when(f == last)
    def _():
        whole(out_hbm.at[all_rows, :], rows, _SEM_ACC).wait()
        y = acc[...] * g_ref[0]
        for k in range(rpt):
            rows[pl.ds(k, tm, stride=rpt), :] += y[:, k * LANES:(k + 1) * LANES]
        start_rows(lambda j, t: pltpu.make_async_copy(slab(rows, j), slab(out_hbm, t), sem.at[_SEM_OUT]))
        whole(rows, out_hbm.at[all_rows, :], _SEM_OUT).wait()


def _moe(idx, g, xn_rs, h_rs, w_gate, w_up, w_down, tf=256):
    E, _, C = idx.shape
    D, F = w_gate.shape[1], w_gate.shape[2]
    rpt = D // LANES
    tm = min(C, 1024)
    any_spec = pl.BlockSpec(memory_space=pl.ANY)
    return pl.pallas_call(
        functools.partial(_moe_body, tm, rpt),
        grid=(E, C // tm, F // tf),
        in_specs=[
            any_spec,
            pl.BlockSpec((1, tm, 1), lambda e, i, f: (e, i, 0)),
            pl.BlockSpec((1, D, tf), lambda e, i, f: (e, 0, f)),
            pl.BlockSpec((1, D, tf), lambda e, i, f: (e, 0, f)),
            pl.BlockSpec((1, tf, D), lambda e, i, f: (e, f, 0)),
            any_spec,
            any_spec,
        ],
        out_specs=any_spec,
        out_shape=jax.ShapeDtypeStruct(h_rs.shape, h_rs.dtype),
        input_output_aliases={6: 0},
        scratch_shapes=[
            pltpu.SMEM((tm,), jnp.int32),
            pltpu.VMEM((tm * rpt, LANES), jnp.float32),
            pltpu.VMEM((tm, D), jnp.bfloat16),
            pltpu.VMEM((tm, D), jnp.float32),
            pltpu.SemaphoreType.DMA((4,)),
        ],
        compiler_params=pltpu.CompilerParams(
            dimension_semantics=("arbitrary", "arbitrary", "arbitrary"),
            vmem_limit_bytes=VMEM_LIMIT_BYTES, has_side_effects=True),
        name="moe",
    )(idx.reshape(E, C // tm, tm), g.reshape(E, C, 1), w_gate, w_up, w_down, xn_rs, h_rs)


def _final_norm_body(x_ref, g_ref, o_ref):
    tm, D = o_ref.shape
    o_ref[...] = _rms(_load_row_split(x_ref, tm, D // LANES), g_ref[...])


def _final_norm(x_rs, g, tm=512):
    D = g.shape[0]
    rpt = D // LANES
    T = x_rs.shape[0] // rpt
    return pl.pallas_call(
        _final_norm_body,
        grid=(T // tm,),
        in_specs=[pl.BlockSpec((tm * rpt, LANES), lambda i: (i, 0)),
                  pl.BlockSpec((1, D), lambda i: (0, 0))],
        out_specs=pl.BlockSpec((tm, D), lambda i: (i, 0)),
        out_shape=jax.ShapeDtypeStruct((T, D), jnp.float32),
        compiler_params=_params("parallel"),
        name="final_norm",
    )(x_rs, g.reshape(1, D))


def _trunk(x, p):
    B, L, D = x.shape
    T = B * L
    xf = x.reshape(T, D)
    proj = _norm_proj(xf, p["norm_mix"], p["w_in"])
    ya, x0, u, ub = _mix_front(proj, p["conv_a_w"], p["conv_h_w"], L)

    taps = _hyena_filter_taps(L, p)
    P, Q, KN = _filter_spectrum(L, taps)
    tc = 1024 if L <= 2048 else 512
    shp = (B, L, D_HYENA)
    yr, z2, yn = _fwd_dft(ub.reshape(shp), P, Q, KN, tf=512, tc=tc)
    yb = _inv_dft(yr, z2, yn, x0.reshape(shp), u.reshape(shp), p["hyena_skip"], tt=512, tc=tc)

    h_rs, xn_rs, aff_t = _merge_out(ya, yb.reshape(T, D_HYENA), proj, xf, p["w_proj_a"],
                                    p["w_proj_h"], p["w_out"], p["norm_ffn"], p["w_router"].T)

    cap = max(1, CAPACITY_FACTOR * T // N_EXPERTS)
    idx, g = _route(aff_t, cap)
    h2_rs = _moe(idx, g, xn_rs, h_rs, p["w_gate"], p["w_up"], p["w_down"])
    return _final_norm(h2_rs, p["norm_final"]).reshape(B, L, D)


def kernel(x_prompt, x_sample, w_in, conv_a_w, conv_h_w, filt_w1, filt_b1, filt_w2, filt_b2, filt_w3, filt_b3, filt_w_out, filt_freq, hyena_skip, w_proj_a, w_proj_h, w_out, norm_mix, norm_ffn, w_router, w_gate, w_up, w_down, norm_final):
    bf16 = jnp.bfloat16
    p = dict(
        w_in=w_in[0].astype(bf16), conv_a_w=conv_a_w[0], conv_h_w=conv_h_w[0],
        filt_w1=filt_w1[0], filt_b1=filt_b1[0], filt_w2=filt_w2[0], filt_b2=filt_b2[0],
        filt_w3=filt_w3[0], filt_b3=filt_b3[0], filt_w_out=filt_w_out[0], filt_freq=filt_freq[0],
        hyena_skip=hyena_skip[0], w_proj_a=w_proj_a[0].astype(bf16), w_proj_h=w_proj_h[0].astype(bf16),
        w_out=w_out[0].astype(bf16), norm_mix=norm_mix[0], norm_ffn=norm_ffn[0],
        w_router=w_router[0], w_gate=w_gate[0], w_up=w_up[0], w_down=w_down[0],
        norm_final=norm_final,
    )
    return (_trunk(x_prompt, p), _trunk(x_sample, p))
```

```python
import functools
import math

import numpy as np

import jax
import jax.numpy as jnp
from jax import lax
from jax.experimental import pallas as pl
from jax.experimental.pallas import tpu as pltpu

D_MODEL = 2048
D_CONV = 1024
D_HYENA = 1024
SHORT_K = 3
FILTER_EMB = 33
FILTER_ORDER = 64
DECAY_FAST = 0.3
DECAY_SLOW = 1.5
DECAY_TARGET = 1e-2
N_EXPERTS = 16
CAPACITY_FACTOR = 2
D_EXPERT = 5632
EPS = 1e-6
D_IN = 3 * D_CONV + 3 * D_HYENA + 2 * D_MODEL
H_OFF = 3 * D_CONV
G_OFF = 3 * D_CONV + 3 * D_HYENA

LANES = 128
SUBLANES = 8
VMEM_LIMIT_BYTES = 56 * 1024 * 1024

_HP = lax.Precision.HIGHEST


def _rms(x, g):
    r = lax.rsqrt(jnp.mean(x * x, axis=-1, keepdims=True) + EPS)
    return (x * r) * g


def _bdot(a, b):
    return jnp.dot(a, b, preferred_element_type=jnp.float32)


def _params(*sem):
    return pltpu.CompilerParams(dimension_semantics=sem, vmem_limit_bytes=VMEM_LIMIT_BYTES)


def _norm_proj_body(x_ref, g_ref, w_ref, o_ref, xn_ref):
    @pl.when(pl.program_id(1) == 0)
    def _():
        xn_ref[...] = _rms(x_ref[...], g_ref[...]).astype(jnp.bfloat16)

    o_ref[...] = _bdot(xn_ref[...], w_ref[...])


def _norm_proj(x, g, w_bf16, tm=1024, tn=1024):
    T, D = x.shape
    N = w_bf16.shape[1]
    return pl.pallas_call(
        _norm_proj_body,
        grid=(T // tm, N // tn),
        in_specs=[
            pl.BlockSpec((tm, D), lambda i, j: (i, 0)),
            pl.BlockSpec((1, D), lambda i, j: (0, 0)),
            pl.BlockSpec((D, tn), lambda i, j: (0, j)),
        ],
        out_specs=pl.BlockSpec((tm, tn), lambda i, j: (i, j)),
        out_shape=jax.ShapeDtypeStruct((T, N), jnp.float32),
        scratch_shapes=[pltpu.VMEM((tm, D), jnp.bfloat16)],
        compiler_params=_params("parallel", "arbitrary"),
        name="norm_proj",
    )(x, g.reshape(1, D), w_bf16)


def _shift_down(x, first_row):
    rows = lax.broadcasted_iota(jnp.int32, x.shape, 0)
    return jnp.where(rows == 0, first_row, pltpu.roll(x, 1, axis=0))


def _shift_up(x, last_row):
    n = x.shape[0]
    rows = lax.broadcasted_iota(jnp.int32, x.shape, 0)
    return jnp.where(rows == n - 1, last_row, pltpu.roll(x, n - 1, axis=0))


def _mix_body(tiles_per_seq, p_ref, pp_ref, pn_ref, wa_ref, wh_ref,
              ya_ref, x0_ref, u_ref, ub_ref):
    i = pl.program_id(0)
    keep_prev = jnp.where(i % tiles_per_seq == 0, 0.0, 1.0)
    keep_next = jnp.where(i % tiles_per_seq == tiles_per_seq - 1, 0.0, 1.0)
    C = D_CONV
    q = p_ref[:, 0:C] * p_ref[:, 2 * C:3 * C]
    q_prev = pp_ref[SUBLANES - 1:SUBLANES, 0:C] * pp_ref[SUBLANES - 1:SUBLANES, 2 * C:3 * C] * keep_prev
    q_next = pn_ref[0:1, 0:C] * pn_ref[0:1, 2 * C:3 * C] * keep_next
    conv = (_shift_down(q, q_prev) * wa_ref[0:1, :] + q * wa_ref[1:2, :]
            + _shift_up(q, q_next) * wa_ref[2:3, :])
    ya_ref[...] = (p_ref[:, C:2 * C] * conv).astype(ya_ref.dtype)

    def hconv(j):
        lo, hi = H_OFF + j * D_HYENA, H_OFF + (j + 1) * D_HYENA
        wlo, whi = j * D_HYENA, (j + 1) * D_HYENA
        x = p_ref[:, lo:hi]
        xp = pp_ref[SUBLANES - 1:SUBLANES, lo:hi] * keep_prev
        xn = pn_ref[0:1, lo:hi] * keep_next
        return (_shift_down(x, xp) * wh_ref[0:1, wlo:whi] + x * wh_ref[1:2, wlo:whi]
                + _shift_up(x, xn) * wh_ref[2:3, wlo:whi])

    x0_ref[...] = hconv(0)
    u = hconv(2) * hconv(1)
    u_ref[...] = u
    ub_ref[...] = u.astype(ub_ref.dtype)


def _mix_front(proj, conv_a_w, conv_h_w, L, tl=256):
    T = proj.shape[0]
    W = G_OFF
    nb = tl // SUBLANES
    last_blk = T // SUBLANES - 1
    return pl.pallas_call(
        functools.partial(_mix_body, L // tl),
        grid=(T // tl,),
        in_specs=[
            pl.BlockSpec((tl, W), lambda i: (i, 0)),
            pl.BlockSpec((SUBLANES, W), lambda i: (jnp.maximum(i * nb - 1, 0), 0)),
            pl.BlockSpec((SUBLANES, W), lambda i: (jnp.minimum((i + 1) * nb, last_blk), 0)),
            pl.BlockSpec((SHORT_K, D_CONV), lambda i: (0, 0)),
            pl.BlockSpec((SHORT_K, 3 * D_HYENA), lambda i: (0, 0)),
        ],
        out_specs=[
            pl.BlockSpec((tl, D_CONV), lambda i: (i, 0)),
            pl.BlockSpec((tl, D_HYENA), lambda i: (i, 0)),
            pl.BlockSpec((tl, D_HYENA), lambda i: (i, 0)),
            pl.BlockSpec((tl, D_HYENA), lambda i: (i, 0)),
        ],
        out_shape=[
            jax.ShapeDtypeStruct((T, D_CONV), jnp.bfloat16),
            jax.ShapeDtypeStruct((T, D_HYENA), jnp.float32),
            jax.ShapeDtypeStruct((T, D_HYENA), jnp.float32),
            jax.ShapeDtypeStruct((T, D_HYENA), jnp.bfloat16),
        ],
        compiler_params=_params("parallel"),
        name="mix_front",
    )(proj, proj, proj, conv_a_w, conv_h_w)


@functools.lru_cache(maxsize=None)
def _dft_tables(L):
    m = (np.arange(L, dtype=np.int64)[:, None] * np.arange(L, dtype=np.int64)[None, :]) % (2 * L)
    ang = m.astype(np.float64) * (math.pi / L)
    c = np.cos(ang).astype(np.float32)
    s = np.sin(ang).astype(np.float32)
    alt = np.zeros((SUBLANES, L), np.float32)
    alt[0] = 1.0 - 2.0 * (np.arange(L) % 2)
    return c.astype(jnp.bfloat16), s.astype(jnp.bfloat16), alt.astype(jnp.bfloat16)


@functools.lru_cache(maxsize=None)
def _filter_consts(L):
    t = np.linspace(0.0, 1.0, L, dtype=np.float32)[:, None]
    bands = (FILTER_EMB - 1) // 2
    ang = (np.float32(2.0 * math.pi / L) * np.arange(L, dtype=np.float32)[:, None]
           * np.linspace(1e-4, bands - 1, bands, dtype=np.float32)[None, :])
    z = np.zeros((L, LANES), np.float32)
    z[:, 0:1] = t
    z[:, 1:1 + bands] = np.cos(ang)
    z[:, 1 + bands:1 + 2 * bands] = -np.sin(ang)
    max_decay = math.log(DECAY_TARGET) / DECAY_FAST
    min_decay = math.log(DECAY_TARGET) / DECAY_SLOW
    deltas = np.abs(np.linspace(min_decay, max_decay, D_HYENA, dtype=np.float32))[None, :]
    return z, deltas


def _filter_body(z_ref, dl_ref, w1_ref, b1_ref, w2_ref, b2_ref, w3_ref, b3_ref, wo_ref, fr_ref,
                 gs_hi_ref, gs_lo_ref, gd_hi_ref, gd_lo_ref):
    z = z_ref[...]
    fr = fr_ref[...]
    h = jnp.sin(fr * (jnp.dot(z, w1_ref[...], precision=_HP, preferred_element_type=jnp.float32) + b1_ref[...]))
    h = jnp.sin(fr * (jnp.dot(h, w2_ref[...], precision=_HP, preferred_element_type=jnp.float32) + b2_ref[...]))
    h = jnp.sin(fr * (jnp.dot(h, w3_ref[...], precision=_HP, preferred_element_type=jnp.float32) + b3_ref[...]))
    h = jnp.dot(h, wo_ref[...], precision=_HP, preferred_element_type=jnp.float32)
    decay = jnp.exp(-z[:, 0:1] * dl_ref[...])
    hf = h[:, :D_HYENA] * decay
    hb = h[:, D_HYENA:] * decay
    rows = lax.broadcasted_iota(jnp.int32, hb.shape, 0) + pl.program_id(0) * hb.shape[0]
    hb = jnp.where(rows == 0, 0.0, hb)
    gs = hf + hb
    gd = hb - hf
    gs_hi = gs.astype(jnp.bfloat16)
    gd_hi = gd.astype(jnp.bfloat16)
    gs_hi_ref[...] = gs_hi
    gd_hi_ref[...] = gd_hi
    gs_lo_ref[...] = (gs - gs_hi.astype(jnp.float32)).astype(jnp.bfloat16)
    gd_lo_ref[...] = (gd - gd_hi.astype(jnp.float32)).astype(jnp.bfloat16)


def _pad_to(a, shape):
    return jnp.pad(a, [(0, n - s) for s, n in zip(a.shape, shape)])


def _hyena_filter_taps(L, p, tl=512):
    z, deltas = _filter_consts(L)
    P = LANES
    w1 = _pad_to(p["filt_w1"], (P, P))
    w2 = _pad_to(p["filt_w2"], (P, P))
    w3 = _pad_to(p["filt_w3"], (P, P))
    wo = _pad_to(p["filt_w_out"], (P, 2 * D_HYENA))
    b1 = _pad_to(p["filt_b1"].reshape(1, -1), (1, P))
    b2 = _pad_to(p["filt_b2"].reshape(1, -1), (1, P))
    b3 = _pad_to(p["filt_b3"].reshape(1, -1), (1, P))
    fr = _pad_to(p["filt_freq"].reshape(1, -1), (1, P))
    const = lambda i: (0, 0)
    out = jax.ShapeDtypeStruct((L, D_HYENA), jnp.bfloat16)
    return pl.pallas_call(
        _filter_body,
        grid=(L // tl,),
        in_specs=[
            pl.BlockSpec((tl, P), lambda i: (i, 0)),
            pl.BlockSpec((1, D_HYENA), const),
            pl.BlockSpec((P, P), const), pl.BlockSpec((1, P), const),
            pl.BlockSpec((P, P), const), pl.BlockSpec((1, P), const),
            pl.BlockSpec((P, P), const), pl.BlockSpec((1, P), const),
            pl.BlockSpec((P, 2 * D_HYENA), const), pl.BlockSpec((1, P), const),
        ],
        out_specs=[pl.BlockSpec((tl, D_HYENA), lambda i: (i, 0))] * 4,
        out_shape=[out] * 4,
        compiler_params=_params("parallel"),
        name="hyena_filter",
    )(z, deltas, w1, b1, w2, b2, w3, b3, wo, fr)


def _spectrum_body(L, c_ref, s_ref, alt_ref, gsh_ref, gsl_ref, gdh_ref, gdl_ref,
                   p_ref, q_ref, kn_ref):
    c = c_ref[...]
    s = s_ref[...]
    tf = c.shape[0]
    f = lax.broadcasted_iota(jnp.int32, (tf, 1), 0) + pl.program_id(1) * tf
    scale = jnp.where(f == 0, 1.0, 2.0) * (1.0 / (2 * L))
    p_ref[...] = (_bdot(c, gsh_ref[...]) + _bdot(c, gsl_ref[...])) * scale
    q_ref[...] = (_bdot(s, gdh_ref[...]) + _bdot(s, gdl_ref[...])) * scale
    alt = alt_ref[...]
    kn_ref[...] = (_bdot(alt, gsh_ref[...]) + _bdot(alt, gsl_ref[...])) * (1.0 / (2 * L))


def _filter_spectrum(L, taps, tf=512, tc=256):
    cmat, smat, alt = _dft_tables(L)
    gcol = pl.BlockSpec((L, tc), lambda j, k: (0, j))
    return pl.pallas_call(
        functools.partial(_spectrum_body, L),
        grid=(D_HYENA // tc, L // tf),
        in_specs=[
            pl.BlockSpec((tf, L), lambda j, k: (k, 0)),
            pl.BlockSpec((tf, L), lambda j, k: (k, 0)),
            pl.BlockSpec((SUBLANES, L), lambda j, k: (0, 0)),
            gcol, gcol, gcol, gcol,
        ],
        out_specs=[
            pl.BlockSpec((tf, tc), lambda j, k: (k, j)),
            pl.BlockSpec((tf, tc), lambda j, k: (k, j)),
            pl.BlockSpec((SUBLANES, tc), lambda j, k: (0, j)),
        ],
        out_shape=[
            jax.ShapeDtypeStruct((L, D_HYENA), jnp.float32),
            jax.ShapeDtypeStruct((L, D_HYENA), jnp.float32),
            jax.ShapeDtypeStruct((SUBLANES, D_HYENA), jnp.float32),
        ],
        compiler_params=_params("parallel", "arbitrary"),
        name="filter_spectrum",
    )(cmat, smat, alt, *taps)


def _fwd_dft_body(c_ref, s_ref, alt_ref, u_ref, p_ref, q_ref, kn_ref, yr_ref, z2_ref, yn_ref):
    u = u_ref[0]
    a = _bdot(c_ref[...], u)
    b = _bdot(s_ref[...], u)
    p = p_ref[...]
    q = q_ref[...]
    yr_ref[0] = (a * p + b * q).astype(yr_ref.dtype)
    z2_ref[0] = (b * p - a * q).astype(z2_ref.dtype)
    yn_ref[0] = _bdot(alt_ref[...], u) * kn_ref[...]


def _fwd_dft(u_bf16, P, Q, KN, tf, tc):
    B, L, C = u_bf16.shape
    cmat, smat, alt = _dft_tables(L)
    return pl.pallas_call(
        _fwd_dft_body,
        grid=(B, C // tc, L // tf),
        in_specs=[
            pl.BlockSpec((tf, L), lambda b, j, k: (k, 0)),
            pl.BlockSpec((tf, L), lambda b, j, k: (k, 0)),
            pl.BlockSpec((SUBLANES, L), lambda b, j, k: (0, 0)),
            pl.BlockSpec((1, L, tc), lambda b, j, k: (b, 0, j)),
            pl.BlockSpec((tf, tc), lambda b, j, k: (k, j)),
            pl.BlockSpec((tf, tc), lambda b, j, k: (k, j)),
            pl.BlockSpec((SUBLANES, tc), lambda b, j, k: (0, j)),
        ],
        out_specs=[
            pl.BlockSpec((1, tf, tc), lambda b, j, k: (b, k, j)),
            pl.BlockSpec((1, tf, tc), lambda b, j, k: (b, k, j)),
            pl.BlockSpec((1, SUBLANES, tc), lambda b, j, k: (b, 0, j)),
        ],
        out_shape=[
            jax.ShapeDtypeStruct((B, L, C), jnp.bfloat16),
            jax.ShapeDtypeStruct((B, L, C), jnp.bfloat16),
            jax.ShapeDtypeStruct((B, SUBLANES, C), jnp.float32),
        ],
        compiler_params=_params("parallel", "parallel", "arbitrary"),
        name="fwd_dft",
    )(cmat, smat, alt, u_bf16, P, Q, KN)


def _inv_dft_body(c_ref, s_ref, yr_ref, z2_ref, yn_ref, x0_ref, u_ref, skip_ref, o_ref):
    y = _bdot(c_ref[...], yr_ref[0]) + _bdot(s_ref[...], z2_ref[0])
    tt = y.shape[0]
    t = lax.broadcasted_iota(jnp.int32, (tt, 1), 0) + pl.program_id(2) * tt
    sign = (1 - 2 * (t & 1)).astype(jnp.float32)
    y = y + sign * yn_ref[0, 0:1, :]
    o_ref[0] = (x0_ref[0] * (y + u_ref[0] * skip_ref[...])).astype(o_ref.dtype)


def _inv_dft(yr, z2, yn, x0, u, skip, tt, tc):
    B, L, C = yr.shape
    cmat, smat, _ = _dft_tables(L)
    return pl.pallas_call(
        _inv_dft_body,
        grid=(B, C // tc, L // tt),
        in_specs=[
            pl.BlockSpec((tt, L), lambda b, j, k: (k, 0)),
            pl.BlockSpec((tt, L), lambda b, j, k: (k, 0)),
            pl.BlockSpec((1, L, tc), lambda b, j, k: (b, 0, j)),
            pl.BlockSpec((1, L, tc), lambda b, j, k: (b, 0, j)),
            pl.BlockSpec((1, SUBLANES, tc), lambda b, j, k: (b, 0, j)),
            pl.BlockSpec((1, tt, tc), lambda b, j, k: (b, k, j)),
            pl.BlockSpec((1, tt, tc), lambda b, j, k: (b, k, j)),
            pl.BlockSpec((1, tc), lambda b, j, k: (0, j)),
        ],
        out_specs=pl.BlockSpec((1, tt, tc), lambda b, j, k: (b, k, j)),
        out_shape=jax.ShapeDtypeStruct((B, L, C), jnp.bfloat16),
        compiler_params=_params("parallel", "parallel", "arbitrary"),
        name="inv_dft",
    )(cmat, smat, yr, z2, yn, x0, u, skip.reshape(1, C))


def _store_row_split(ref, x):
    rpt = x.shape[1] // LANES
    n = x.shape[0]
    for k in range(rpt):
        ref[pl.ds(k, n, stride=rpt), :] = x[:, k * LANES:(k + 1) * LANES]


def _load_row_split(ref, n, rpt):
    return jnp.concatenate([ref[pl.ds(k, n, stride=rpt), :] for k in range(rpt)], axis=1)


def _branch_merge_body(ya_ref, yb_ref, ga_ref, gb_ref, wa_ref, wh_ref, m_ref):
    pa = _bdot(ya_ref[...], wa_ref[...])
    ph = _bdot(yb_ref[...], wh_ref[...])
    merged = jax.nn.sigmoid(ga_ref[...]) * pa + jax.nn.sigmoid(gb_ref[...]) * ph
    m_ref[...] = merged.astype(m_ref.dtype)


def _branch_merge(ya, yb, proj, wa, wh, tm=512):
    T = ya.shape[0]
    D = wa.shape[1]
    const = lambda i: (0, 0)
    row = lambda i: (i, 0)
    one = pl.Buffered(1)
    ga_blk = G_OFF // D
    return pl.pallas_call(
        _branch_merge_body,
        grid=(T // tm,),
        in_specs=[
            pl.BlockSpec((tm, D_CONV), row),
            pl.BlockSpec((tm, D_HYENA), row),
            pl.BlockSpec((tm, D), lambda i: (i, ga_blk)),
            pl.BlockSpec((tm, D), lambda i: (i, ga_blk + 1)),
            pl.BlockSpec((D_CONV, D), const, pipeline_mode=one),
            pl.BlockSpec((D_HYENA, D), const, pipeline_mode=one),
        ],
        out_specs=pl.BlockSpec((tm, D), row),
        out_shape=jax.ShapeDtypeStruct((T, D), jnp.bfloat16),
        compiler_params=_params("parallel"),
        name="branch_merge",
    )(ya, yb, proj, proj, wa, wh)


def _merge_body(m_ref, x_ref, wo_ref, gn_ref, wrt_ref, h_ref, xn_ref, aff_ref):
    h = x_ref[...] + _bdot(m_ref[...], wo_ref[...])
    _store_row_split(h_ref, h)
    xn = _rms(h, gn_ref[...])
    _store_row_split(xn_ref, xn)
    lg = lax.dot_general(wrt_ref[...], xn, (((1,), (1,)), ((), ())),
                         preferred_element_type=jnp.float32, precision=_HP)
    ex = jnp.exp(lg - jnp.max(lg, axis=0, keepdims=True))
    aff_ref[...] = ex / jnp.sum(ex, axis=0, keepdims=True)


def _merge_out(merged, x, wo, g_ffn, w_router_t, tm=512):
    T, D = x.shape
    rpt = D // LANES
    const = lambda i: (0, 0)
    row = lambda i: (i, 0)
    one = pl.Buffered(1)
    return pl.pallas_call(
        _merge_body,
        grid=(T // tm,),
        in_specs=[
            pl.BlockSpec((tm, D), row),
            pl.BlockSpec((tm, D), row),
            pl.BlockSpec((D, D), const, pipeline_mode=one),
            pl.BlockSpec((1, D), const, pipeline_mode=one),
            pl.BlockSpec((N_EXPERTS, D), const, pipeline_mode=one),
        ],
        out_specs=[
            pl.BlockSpec((tm * rpt, LANES), row),
            pl.BlockSpec((tm * rpt, LANES), row),
            pl.BlockSpec((N_EXPERTS, tm), lambda i: (0, i)),
        ],
        out_shape=[
            jax.ShapeDtypeStruct((T * rpt, LANES), jnp.float32),
            jax.ShapeDtypeStruct((T * rpt, LANES), jnp.float32),
            jax.ShapeDtypeStruct((N_EXPERTS, T), jnp.float32),
        ],
        compiler_params=_params("parallel"),
        name="merge_out",
    )(merged, x, wo, g_ffn.reshape(1, D), w_router_t)


def _route_body(cap, aff_ref, idx_ref, g_ref):
    f32, bf16 = jnp.float32, jnp.bfloat16
    x = aff_ref[0]
    R = x.shape[0]
    bits = pltpu.bitcast(x, jnp.int32)

    def count(m):
        return jnp.sum(jnp.where(m, 1.0, 0.0), keepdims=True)

    def bit_step(i, cur):
        cand = cur | jnp.left_shift(jnp.int32(1), 30 - i)
        return jnp.where(count(bits >= cand) >= cap, cand, cur)

    thr = lax.fori_loop(0, 31, bit_step, jnp.zeros((1, 1), jnp.int32))

    def tri(n, cmp):
        a = lax.broadcasted_iota(jnp.int32, (n, n), 0)
        b = lax.broadcasted_iota(jnp.int32, (n, n), 1)
        return jnp.where(cmp(a, b), 1.0, 0.0).astype(bf16)

    upper = tri(LANES, lambda k, l: k <= l)
    ones = jnp.ones((LANES, LANES), bf16)
    strict_lower = tri(R, lambda r, k: k < r)

    def prefix(m01):
        mb = m01.astype(bf16)
        rowcum = _bdot(mb, upper)
        rowtot = _bdot(mb, ones)
        before = _bdot(strict_lower, rowtot.astype(bf16))
        return mb, rowcum, rowtot, before

    gt = jnp.where(bits > thr, 1.0, 0.0)
    eq = jnp.where(bits == thr, 1.0, 0.0)
    _, eq_cum, _, eq_before = prefix(eq)
    need = cap - jnp.sum(gt, keepdims=True)
    sel = gt + eq * jnp.where(eq_cum + eq_before <= need, 1.0, 0.0)
    selb, rowcum, rowtot, before = prefix(sel)

    through_row = (before + rowtot)[:, 0:1]
    j = lax.broadcasted_iota(jnp.int32, (1, cap), 1).astype(f32)
    done = through_row <= j
    r_j = jnp.sum(jnp.where(done, 1.0, 0.0), axis=0, keepdims=True)
    before_j = jnp.sum(jnp.where(done, rowtot[:, 0:1], 0.0), axis=0, keepdims=True)
    rows = lax.broadcasted_iota(jnp.int32, (R, cap), 0).astype(f32)
    pick_row = jnp.where(rows == r_j, 1.0, 0.0).astype(bf16)

    nt = (((1,), (1,)), ((), ()))
    lower = tri(LANES, lambda l, k: k <= l)
    rowcum_t = lax.dot_general(lower, selb, nt, preferred_element_type=f32)
    cum_j = _bdot(rowcum_t.astype(bf16), pick_row)
    lane_j = jnp.sum(jnp.where(cum_j <= j - before_j, 1.0, 0.0), axis=0, keepdims=True)
    idx_ref[0] = (r_j * LANES + lane_j).astype(jnp.int32)

    eye = tri(LANES, lambda a, b: a == b)
    hi = x.astype(bf16)
    r1 = x - hi.astype(f32)
    mid = r1.astype(bf16)
    lo = (r1 - mid.astype(f32)).astype(bf16)

    def pick(part):
        part_t = lax.dot_general(eye, part, nt, preferred_element_type=f32)
        return _bdot(part_t.astype(bf16), pick_row)

    aff_j = (pick(hi) + pick(mid)) + pick(lo)
    lanes = lax.broadcasted_iota(jnp.int32, (LANES, cap), 0).astype(f32)
    g_ref[0] = jnp.sum(jnp.where(lanes == lane_j, aff_j, 0.0), axis=0, keepdims=True)


def _route(aff_t, cap):
    E, T = aff_t.shape
    R = T // LANES
    return pl.pallas_call(
        functools.partial(_route_body, cap),
        grid=(E,),
        in_specs=[pl.BlockSpec((1, R, LANES), lambda e: (e, 0, 0))],
        out_specs=[pl.BlockSpec((1, 1, cap), lambda e: (e, 0, 0)),
                   pl.BlockSpec((1, 1, cap), lambda e: (e, 0, 0))],
        out_shape=[jax.ShapeDtypeStruct((E, 1, cap), jnp.int32),
                   jax.ShapeDtypeStruct((E, 1, cap), jnp.float32)],
        compiler_params=_params("parallel"),
        name="route",
    )(aff_t.reshape(E, R, LANES))


_SEM_X, _SEM_ACC, _SEM_OUT, _SEM_IDX = range(4)


def _moe_body(tm, rpt, n_tiles, n_f, idx_hbm, g_ref, wg_ref, wu_ref, wd_ref, xn_hbm, h_hbm, out_hbm,
              idx_smem, xrows, arows, xb, acc, sem):
    del h_hbm
    n = pl.program_id(0) * pl.num_programs(1) + pl.program_id(1)
    f = pl.program_id(2)
    last = n_f - 1
    mid = min(2, last)
    slot = n % 2

    def slab(ref, r):
        return ref.at[pl.ds(pl.multiple_of(r * rpt, rpt), rpt), :]

    def load_idx(tile, s):
        cp = pltpu.make_async_copy(idx_hbm.at[tile], idx_smem.at[s], sem.at[_SEM_IDX])
        cp.start()
        cp.wait()

    def start_rows(s, make):
        def body(j, c):
            make(j, idx_smem[s, j]).start()
            return c
        lax.fori_loop(0, tm, body, 0, unroll=8)

    def gather_x(s):
        start_rows(s, lambda j, t: pltpu.make_async_copy(slab(xn_hbm, t), slab(xrows, j), sem.at[_SEM_X]))

    all_rows = pl.ds(0, tm * rpt)
    x_landed = pltpu.make_async_copy(xn_hbm.at[all_rows, :], xrows, sem.at[_SEM_X])
    acc_landed = pltpu.make_async_copy(out_hbm.at[all_rows, :], arows, sem.at[_SEM_ACC])
    out_landed = pltpu.make_async_copy(arows, out_hbm.at[all_rows, :], sem.at[_SEM_OUT])

    @pl.when(f == 0)
    def _():
        @pl.when(n == 0)
        def _():
            load_idx(0, 0)
            gather_x(0)

        x_landed.wait()
        xb[...] = _load_row_split(xrows, tm, rpt).astype(jnp.bfloat16)
        acc[...] = jnp.zeros_like(acc)

    @pl.when(f == mid)
    def _():
        @pl.when(n > 0)
        def _():
            out_landed.wait()

        start_rows(slot, lambda j, t: pltpu.make_async_copy(slab(out_hbm, t), slab(arows, j), sem.at[_SEM_ACC]))

        @pl.when(n + 1 < n_tiles)
        def _():
            load_idx(n + 1, 1 - slot)
            gather_x(1 - slot)

    x = xb[...]
    a = _bdot(x, wg_ref[0].astype(jnp.bfloat16))
    b = _bdot(x, wu_ref[0].astype(jnp.bfloat16))
    hmid = (a * jax.nn.sigmoid(a) * b).astype(jnp.bfloat16)
    acc[...] += _bdot(hmid, wd_ref[0].astype(jnp.bfloat16))

    @pl.when(f == last)
    def _():
        acc_landed.wait()
        g = g_ref[0]
        for k in range(rpt):
            arows[pl.ds(k, tm, stride=rpt), :] += acc[:, k * LANES:(k + 1) * LANES] * g
        start_rows(slot, lambda j, t: pltpu.make_async_copy(slab(arows, j), slab(out_hbm, t), sem.at[_SEM_OUT]))

        @pl.when(n == n_tiles - 1)
        def _():
            out_landed.wait()


def _moe(idx, g, xn_rs, h_rs, w_gate, w_up, w_down, tf=256):
    E, _, C = idx.shape
    D, F = w_gate.shape[1], w_gate.shape[2]
    rpt = D // LANES
    tm = min(C, 1024)
    any_spec = pl.BlockSpec(memory_space=pl.ANY)
    n_tiles = E * (C // tm)
    return pl.pallas_call(
        functools.partial(_moe_body, tm, rpt, n_tiles, F // tf),
        grid=(E, C // tm, F // tf),
        in_specs=[
            any_spec,
            pl.BlockSpec((1, tm, 1), lambda e, i, f: (e, i, 0)),
            pl.BlockSpec((1, D, tf), lambda e, i, f: (e, 0, f)),
            pl.BlockSpec((1, D, tf), lambda e, i, f: (e, 0, f)),
            pl.BlockSpec((1, tf, D), lambda e, i, f: (e, f, 0)),
            any_spec,
            any_spec,
        ],
        out_specs=any_spec,
        out_shape=jax.ShapeDtypeStruct(h_rs.shape, h_rs.dtype),
        input_output_aliases={6: 0},
        scratch_shapes=[
            pltpu.SMEM((2, tm), jnp.int32),
            pltpu.VMEM((tm * rpt, LANES), jnp.float32),
            pltpu.VMEM((tm * rpt, LANES), jnp.float32),
            pltpu.VMEM((tm, D), jnp.bfloat16),
            pltpu.VMEM((tm, D), jnp.float32),
            pltpu.SemaphoreType.DMA((4,)),
        ],
        compiler_params=pltpu.CompilerParams(
            dimension_semantics=("arbitrary", "arbitrary", "arbitrary"),
            vmem_limit_bytes=VMEM_LIMIT_BYTES, has_side_effects=True),
        name="moe",
    )(idx.reshape(n_tiles, tm), g.reshape(E, C, 1), w_gate, w_up, w_down, xn_rs, h_rs)


def _final_norm_body(x_ref, g_ref, o_ref):
    tm, D = o_ref.shape
    o_ref[...] = _rms(_load_row_split(x_ref, tm, D // LANES), g_ref[...])


def _final_norm(x_rs, g, tm=512):
    D = g.shape[0]
    rpt = D // LANES
    T = x_rs.shape[0] // rpt
    return pl.pallas_call(
        _final_norm_body,
        grid=(T // tm,),
        in_specs=[pl.BlockSpec((tm * rpt, LANES), lambda i: (i, 0)),
                  pl.BlockSpec((1, D), lambda i: (0, 0))],
        out_specs=pl.BlockSpec((tm, D), lambda i: (i, 0)),
        out_shape=jax.ShapeDtypeStruct((T, D), jnp.float32),
        compiler_params=_params("parallel"),
        name="final_norm",
    )(x_rs, g.reshape(1, D))


def _trunk(x, p):
    B, L, D = x.shape
    T = B * L
    xf = x.reshape(T, D)
    proj = _norm_proj(xf, p["norm_mix"], p["w_in"])
    ya, x0, u, ub = _mix_front(proj, p["conv_a_w"], p["conv_h_w"], L)

    taps = _hyena_filter_taps(L, p)
    P, Q, KN = _filter_spectrum(L, taps)
    tc = 1024 if L <= 2048 else 512
    shp = (B, L, D_HYENA)
    yr, z2, yn = _fwd_dft(ub.reshape(shp), P, Q, KN, tf=512, tc=tc)
    yb = _inv_dft(yr, z2, yn, x0.reshape(shp), u.reshape(shp), p["hyena_skip"], tt=512, tc=tc)

    merged = _branch_merge(ya, yb.reshape(T, D_HYENA), proj, p["w_proj_a"], p["w_proj_h"])
    h_rs, xn_rs, aff_t = _merge_out(merged, xf, p["w_out"], p["norm_ffn"], p["w_router"].T)

    cap = max(1, CAPACITY_FACTOR * T // N_EXPERTS)
    idx, g = _route(aff_t, cap)
    h2_rs = _moe(idx, g, xn_rs, h_rs, p["w_gate"], p["w_up"], p["w_down"])
    return _final_norm(h2_rs, p["norm_final"]).reshape(B, L, D)


def kernel(x_prompt, x_sample, w_in, conv_a_w, conv_h_w, filt_w1, filt_b1, filt_w2, filt_b2, filt_w3, filt_b3, filt_w_out, filt_freq, hyena_skip, w_proj_a, w_proj_h, w_out, norm_mix, norm_ffn, w_router, w_gate, w_up, w_down, norm_final):
    bf16 = jnp.bfloat16
    p = dict(
        w_in=w_in[0].astype(bf16), conv_a_w=conv_a_w[0], conv_h_w=conv_h_w[0],
        filt_w1=filt_w1[0], filt_b1=filt_b1[0], filt_w2=filt_w2[0], filt_b2=filt_b2[0],
        filt_w3=filt_w3[0], filt_b3=filt_b3[0], filt_w_out=filt_w_out[0], filt_freq=filt_freq[0],
        hyena_skip=hyena_skip[0], w_proj_a=w_proj_a[0].astype(bf16), w_proj_h=w_proj_h[0].astype(bf16),
        w_out=w_out[0].astype(bf16), norm_mix=norm_mix[0], norm_ffn=norm_ffn[0],
        w_router=w_router[0], w_gate=w_gate[0], w_up=w_up[0], w_down=w_down[0],
        norm_final=norm_final,
    )
    return (_trunk(x_prompt, p), _trunk(x_sample, p))
```

```python
import functools
import math

import numpy as np

import jax
import jax.numpy as jnp
from jax import lax
from jax.experimental import pallas as pl
from jax.experimental.pallas import tpu as pltpu

D_MODEL = 2048
D_CONV = 1024
D_HYENA = 1024
SHORT_K = 3
FILTER_EMB = 33
FILTER_ORDER = 64
DECAY_FAST = 0.3
DECAY_SLOW = 1.5
DECAY_TARGET = 1e-2
N_EXPERTS = 16
CAPACITY_FACTOR = 2
D_EXPERT = 5632
EPS = 1e-6
D_IN = 3 * D_CONV + 3 * D_HYENA + 2 * D_MODEL
H_OFF = 3 * D_CONV
G_OFF = 3 * D_CONV + 3 * D_HYENA

LANES = 128
SUBLANES = 8
VMEM_LIMIT_BYTES = 56 * 1024 * 1024

_HP = lax.Precision.HIGHEST


def _rms(x, g):
    r = lax.rsqrt(jnp.mean(x * x, axis=-1, keepdims=True) + EPS)
    return (x * r) * g


def _bdot(a, b):
    return jnp.dot(a, b, preferred_element_type=jnp.float32)


def _params(*sem):
    return pltpu.CompilerParams(dimension_semantics=sem, vmem_limit_bytes=VMEM_LIMIT_BYTES)


def _norm_proj_body(x_ref, g_ref, w_ref, o_ref, xn_ref):
    @pl.when(pl.program_id(1) == 0)
    def _():
        xn_ref[...] = _rms(x_ref[...], g_ref[...]).astype(jnp.bfloat16)

    o_ref[...] = _bdot(xn_ref[...], w_ref[...])


def _norm_proj(x, g, w_bf16, tm=1024, tn=1024):
    T, D = x.shape
    N = w_bf16.shape[1]
    return pl.pallas_call(
        _norm_proj_body,
        grid=(T // tm, N // tn),
        in_specs=[
            pl.BlockSpec((tm, D), lambda i, j: (i, 0)),
            pl.BlockSpec((1, D), lambda i, j: (0, 0)),
            pl.BlockSpec((D, tn), lambda i, j: (0, j)),
        ],
        out_specs=pl.BlockSpec((tm, tn), lambda i, j: (i, j)),
        out_shape=jax.ShapeDtypeStruct((T, N), jnp.float32),
        scratch_shapes=[pltpu.VMEM((tm, D), jnp.bfloat16)],
        compiler_params=_params("parallel", "arbitrary"),
        name="norm_proj",
    )(x, g.reshape(1, D), w_bf16)


def _shift_down(x, first_row):
    rows = lax.broadcasted_iota(jnp.int32, x.shape, 0)
    return jnp.where(rows == 0, first_row, pltpu.roll(x, 1, axis=0))


def _shift_up(x, last_row):
    n = x.shape[0]
    rows = lax.broadcasted_iota(jnp.int32, x.shape, 0)
    return jnp.where(rows == n - 1, last_row, pltpu.roll(x, n - 1, axis=0))


def _mix_body(tiles_per_seq, p_ref, pp_ref, pn_ref, wa_ref, wh_ref,
              ya_ref, x0_ref, u_ref, ub_ref):
    i = pl.program_id(0)
    keep_prev = jnp.where(i % tiles_per_seq == 0, 0.0, 1.0)
    keep_next = jnp.where(i % tiles_per_seq == tiles_per_seq - 1, 0.0, 1.0)
    C = D_CONV
    q = p_ref[:, 0:C] * p_ref[:, 2 * C:3 * C]
    q_prev = pp_ref[SUBLANES - 1:SUBLANES, 0:C] * pp_ref[SUBLANES - 1:SUBLANES, 2 * C:3 * C] * keep_prev
    q_next = pn_ref[0:1, 0:C] * pn_ref[0:1, 2 * C:3 * C] * keep_next
    conv = (_shift_down(q, q_prev) * wa_ref[0:1, :] + q * wa_ref[1:2, :]
            + _shift_up(q, q_next) * wa_ref[2:3, :])
    ya_ref[...] = (p_ref[:, C:2 * C] * conv).astype(ya_ref.dtype)

    def hconv(j):
        lo, hi = H_OFF + j * D_HYENA, H_OFF + (j + 1) * D_HYENA
        wlo, whi = j * D_HYENA, (j + 1) * D_HYENA
        x = p_ref[:, lo:hi]
        xp = pp_ref[SUBLANES - 1:SUBLANES, lo:hi] * keep_prev
        xn = pn_ref[0:1, lo:hi] * keep_next
        return (_shift_down(x, xp) * wh_ref[0:1, wlo:whi] + x * wh_ref[1:2, wlo:whi]
                + _shift_up(x, xn) * wh_ref[2:3, wlo:whi])

    x0_ref[...] = hconv(0)
    u = hconv(2) * hconv(1)
    u_ref[...] = u
    ub_ref[...] = u.astype(ub_ref.dtype)


def _mix_front(proj, conv_a_w, conv_h_w, L, tl=256):
    T = proj.shape[0]
    W = G_OFF
    nb = tl // SUBLANES
    last_blk = T // SUBLANES - 1
    return pl.pallas_call(
        functools.partial(_mix_body, L // tl),
        grid=(T // tl,),
        in_specs=[
            pl.BlockSpec((tl, W), lambda i: (i, 0)),
            pl.BlockSpec((SUBLANES, W), lambda i: (jnp.maximum(i * nb - 1, 0), 0)),
            pl.BlockSpec((SUBLANES, W), lambda i: (jnp.minimum((i + 1) * nb, last_blk), 0)),
            pl.BlockSpec((SHORT_K, D_CONV), lambda i: (0, 0)),
            pl.BlockSpec((SHORT_K, 3 * D_HYENA), lambda i: (0, 0)),
        ],
        out_specs=[
            pl.BlockSpec((tl, D_CONV), lambda i: (i, 0)),
            pl.BlockSpec((tl, D_HYENA), lambda i: (i, 0)),
            pl.BlockSpec((tl, D_HYENA), lambda i: (i, 0)),
            pl.BlockSpec((tl, D_HYENA), lambda i: (i, 0)),
        ],
        out_shape=[
            jax.ShapeDtypeStruct((T, D_CONV), jnp.bfloat16),
            jax.ShapeDtypeStruct((T, D_HYENA), jnp.float32),
            jax.ShapeDtypeStruct((T, D_HYENA), jnp.float32),
            jax.ShapeDtypeStruct((T, D_HYENA), jnp.bfloat16),
        ],
        compiler_params=_params("parallel"),
        name="mix_front",
    )(proj, proj, proj, conv_a_w, conv_h_w)


@functools.lru_cache(maxsize=None)
def _dft_tables(L):
    m = (np.arange(L, dtype=np.int64)[:, None] * np.arange(L, dtype=np.int64)[None, :]) % (2 * L)
    ang = m.astype(np.float64) * (math.pi / L)
    c = np.cos(ang).astype(np.float32)
    s = np.sin(ang).astype(np.float32)
    alt = np.zeros((SUBLANES, L), np.float32)
    alt[0] = 1.0 - 2.0 * (np.arange(L) % 2)
    return c.astype(jnp.bfloat16), s.astype(jnp.bfloat16), alt.astype(jnp.bfloat16)


@functools.lru_cache(maxsize=None)
def _filter_consts(L):
    t = np.linspace(0.0, 1.0, L, dtype=np.float32)[:, None]
    bands = (FILTER_EMB - 1) // 2
    ang = (np.float32(2.0 * math.pi / L) * np.arange(L, dtype=np.float32)[:, None]
           * np.linspace(1e-4, bands - 1, bands, dtype=np.float32)[None, :])
    z = np.zeros((L, LANES), np.float32)
    z[:, 0:1] = t
    z[:, 1:1 + bands] = np.cos(ang)
    z[:, 1 + bands:1 + 2 * bands] = -np.sin(ang)
    max_decay = math.log(DECAY_TARGET) / DECAY_FAST
    min_decay = math.log(DECAY_TARGET) / DECAY_SLOW
    deltas = np.abs(np.linspace(min_decay, max_decay, D_HYENA, dtype=np.float32))[None, :]
    return z, deltas


def _filter_body(z_ref, dl_ref, w1_ref, b1_ref, w2_ref, b2_ref, w3_ref, b3_ref, wo_ref, fr_ref,
                 gs_hi_ref, gs_lo_ref, gd_hi_ref, gd_lo_ref):
    z = z_ref[...]
    fr = fr_ref[...]
    h = jnp.sin(fr * (jnp.dot(z, w1_ref[...], precision=_HP, preferred_element_type=jnp.float32) + b1_ref[...]))
    h = jnp.sin(fr * (jnp.dot(h, w2_ref[...], precision=_HP, preferred_element_type=jnp.float32) + b2_ref[...]))
    h = jnp.sin(fr * (jnp.dot(h, w3_ref[...], precision=_HP, preferred_element_type=jnp.float32) + b3_ref[...]))
    h = jnp.dot(h, wo_ref[...], precision=_HP, preferred_element_type=jnp.float32)
    decay = jnp.exp(-z[:, 0:1] * dl_ref[...])
    hf = h[:, :D_HYENA] * decay
    hb = h[:, D_HYENA:] * decay
    rows = lax.broadcasted_iota(jnp.int32, hb.shape, 0) + pl.program_id(0) * hb.shape[0]
    hb = jnp.where(rows == 0, 0.0, hb)
    gs = hf + hb
    gd = hb - hf
    gs_hi = gs.astype(jnp.bfloat16)
    gd_hi = gd.astype(jnp.bfloat16)
    gs_hi_ref[...] = gs_hi
    gd_hi_ref[...] = gd_hi
    gs_lo_ref[...] = (gs - gs_hi.astype(jnp.float32)).astype(jnp.bfloat16)
    gd_lo_ref[...] = (gd - gd_hi.astype(jnp.float32)).astype(jnp.bfloat16)


def _pad_to(a, shape):
    return jnp.pad(a, [(0, n - s) for s, n in zip(a.shape, shape)])


def _hyena_filter_taps(L, p, tl=512):
    z, deltas = _filter_consts(L)
    P = LANES
    w1 = _pad_to(p["filt_w1"], (P, P))
    w2 = _pad_to(p["filt_w2"], (P, P))
    w3 = _pad_to(p["filt_w3"], (P, P))
    wo = _pad_to(p["filt_w_out"], (P, 2 * D_HYENA))
    b1 = _pad_to(p["filt_b1"].reshape(1, -1), (1, P))
    b2 = _pad_to(p["filt_b2"].reshape(1, -1), (1, P))
    b3 = _pad_to(p["filt_b3"].reshape(1, -1), (1, P))
    fr = _pad_to(p["filt_freq"].reshape(1, -1), (1, P))
    const = lambda i: (0, 0)
    out = jax.ShapeDtypeStruct((L, D_HYENA), jnp.bfloat16)
    return pl.pallas_call(
        _filter_body,
        grid=(L // tl,),
        in_specs=[
            pl.BlockSpec((tl, P), lambda i: (i, 0)),
            pl.BlockSpec((1, D_HYENA), const),
            pl.BlockSpec((P, P), const), pl.BlockSpec((1, P), const),
            pl.BlockSpec((P, P), const), pl.BlockSpec((1, P), const),
            pl.BlockSpec((P, P), const), pl.BlockSpec((1, P), const),
            pl.BlockSpec((P, 2 * D_HYENA), const), pl.BlockSpec((1, P), const),
        ],
        out_specs=[pl.BlockSpec((tl, D_HYENA), lambda i: (i, 0))] * 4,
        out_shape=[out] * 4,
        compiler_params=_params("parallel"),
        name="hyena_filter",
    )(z, deltas, w1, b1, w2, b2, w3, b3, wo, fr)


def _spectrum_body(L, c_ref, s_ref, alt_ref, gsh_ref, gsl_ref, gdh_ref, gdl_ref,
                   p_ref, q_ref, kn_ref):
    c = c_ref[...]
    s = s_ref[...]
    tf = c.shape[0]
    f = lax.broadcasted_iota(jnp.int32, (tf, 1), 0) + pl.program_id(1) * tf
    scale = jnp.where(f == 0, 1.0, 2.0) * (1.0 / (2 * L))
    p_ref[...] = (_bdot(c, gsh_ref[...]) + _bdot(c, gsl_ref[...])) * scale
    q_ref[...] = (_bdot(s, gdh_ref[...]) + _bdot(s, gdl_ref[...])) * scale
    alt = alt_ref[...]
    kn_ref[...] = (_bdot(alt, gsh_ref[...]) + _bdot(alt, gsl_ref[...])) * (1.0 / (2 * L))


def _filter_spectrum(L, taps, tf=512, tc=256):
    cmat, smat, alt = _dft_tables(L)
    gcol = pl.BlockSpec((L, tc), lambda j, k: (0, j))
    return pl.pallas_call(
        functools.partial(_spectrum_body, L),
        grid=(D_HYENA // tc, L // tf),
        in_specs=[
            pl.BlockSpec((tf, L), lambda j, k: (k, 0)),
            pl.BlockSpec((tf, L), lambda j, k: (k, 0)),
            pl.BlockSpec((SUBLANES, L), lambda j, k: (0, 0)),
            gcol, gcol, gcol, gcol,
        ],
        out_specs=[
            pl.BlockSpec((tf, tc), lambda j, k: (k, j)),
            pl.BlockSpec((tf, tc), lambda j, k: (k, j)),
            pl.BlockSpec((SUBLANES, tc), lambda j, k: (0, j)),
        ],
        out_shape=[
            jax.ShapeDtypeStruct((L, D_HYENA), jnp.float32),
            jax.ShapeDtypeStruct((L, D_HYENA), jnp.float32),
            jax.ShapeDtypeStruct((SUBLANES, D_HYENA), jnp.float32),
        ],
        compiler_params=_params("parallel", "arbitrary"),
        name="filter_spectrum",
    )(cmat, smat, alt, *taps)


def _fwd_dft_body(c_ref, s_ref, alt_ref, u_ref, p_ref, q_ref, kn_ref, yr_ref, z2_ref, yn_ref):
    u = u_ref[0]
    a = _bdot(c_ref[...], u)
    b = _bdot(s_ref[...], u)
    p = p_ref[...]
    q = q_ref[...]
    yr_ref[0] = (a * p + b * q).astype(yr_ref.dtype)
    z2_ref[0] = (b * p - a * q).astype(z2_ref.dtype)
    yn_ref[0] = _bdot(alt_ref[...], u) * kn_ref[...]


def _fwd_dft(u_bf16, P, Q, KN, tf, tc):
    B, L, C = u_bf16.shape
    cmat, smat, alt = _dft_tables(L)
    return pl.pallas_call(
        _fwd_dft_body,
        grid=(B, C // tc, L // tf),
        in_specs=[
            pl.BlockSpec((tf, L), lambda b, j, k: (k, 0)),
            pl.BlockSpec((tf, L), lambda b, j, k: (k, 0)),
            pl.BlockSpec((SUBLANES, L), lambda b, j, k: (0, 0)),
            pl.BlockSpec((1, L, tc), lambda b, j, k: (b, 0, j)),
            pl.BlockSpec((tf, tc), lambda b, j, k: (k, j)),
            pl.BlockSpec((tf, tc), lambda b, j, k: (k, j)),
            pl.BlockSpec((SUBLANES, tc), lambda b, j, k: (0, j)),
        ],
        out_specs=[
            pl.BlockSpec((1, tf, tc), lambda b, j, k: (b, k, j)),
            pl.BlockSpec((1, tf, tc), lambda b, j, k: (b, k, j)),
            pl.BlockSpec((1, SUBLANES, tc), lambda b, j, k: (b, 0, j)),
        ],
        out_shape=[
            jax.ShapeDtypeStruct((B, L, C), jnp.bfloat16),
            jax.ShapeDtypeStruct((B, L, C), jnp.bfloat16),
            jax.ShapeDtypeStruct((B, SUBLANES, C), jnp.float32),
        ],
        compiler_params=_params("parallel", "parallel", "arbitrary"),
        name="fwd_dft",
    )(cmat, smat, alt, u_bf16, P, Q, KN)


def _inv_dft_body(c_ref, s_ref, yr_ref, z2_ref, yn_ref, x0_ref, u_ref, skip_ref, o_ref):
    y = _bdot(c_ref[...], yr_ref[0]) + _bdot(s_ref[...], z2_ref[0])
    tt = y.shape[0]
    t = lax.broadcasted_iota(jnp.int32, (tt, 1), 0) + pl.program_id(2) * tt
    sign = (1 - 2 * (t & 1)).astype(jnp.float32)
    y = y + sign * yn_ref[0, 0:1, :]
    o_ref[0] = (x0_ref[0] * (y + u_ref[0] * skip_ref[...])).astype(o_ref.dtype)


def _inv_dft(yr, z2, yn, x0, u, skip, tt, tc):
    B, L, C = yr.shape
    cmat, smat, _ = _dft_tables(L)
    return pl.pallas_call(
        _inv_dft_body,
        grid=(B, C // tc, L // tt),
        in_specs=[
            pl.BlockSpec((tt, L), lambda b, j, k: (k, 0)),
            pl.BlockSpec((tt, L), lambda b, j, k: (k, 0)),
            pl.BlockSpec((1, L, tc), lambda b, j, k: (b, 0, j)),
            pl.BlockSpec((1, L, tc), lambda b, j, k: (b, 0, j)),
            pl.BlockSpec((1, SUBLANES, tc), lambda b, j, k: (b, 0, j)),
            pl.BlockSpec((1, tt, tc), lambda b, j, k: (b, k, j)),
            pl.BlockSpec((1, tt, tc), lambda b, j, k: (b, k, j)),
            pl.BlockSpec((1, tc), lambda b, j, k: (0, j)),
        ],
        out_specs=pl.BlockSpec((1, tt, tc), lambda b, j, k: (b, k, j)),
        out_shape=jax.ShapeDtypeStruct((B, L, C), jnp.bfloat16),
        compiler_params=_params("parallel", "parallel", "arbitrary"),
        name="inv_dft",
    )(cmat, smat, yr, z2, yn, x0, u, skip.reshape(1, C))


def _store_row_split(ref, x):
    rpt = x.shape[1] // LANES
    n = x.shape[0]
    for k in range(rpt):
        ref[pl.ds(k, n, stride=rpt), :] = x[:, k * LANES:(k + 1) * LANES]


def _load_row_split(ref, n, rpt):
    return jnp.concatenate([ref[pl.ds(k, n, stride=rpt), :] for k in range(rpt)], axis=1)


def _branch_merge_body(ya_ref, yb_ref, ga_ref, gb_ref, wa_ref, wh_ref, m_ref):
    pa = _bdot(ya_ref[...], wa_ref[...])
    ph = _bdot(yb_ref[...], wh_ref[...])
    merged = jax.nn.sigmoid(ga_ref[...]) * pa + jax.nn.sigmoid(gb_ref[...]) * ph
    m_ref[...] = merged.astype(m_ref.dtype)


def _branch_merge(ya, yb, proj, wa, wh, tm=512):
    T = ya.shape[0]
    D = wa.shape[1]
    const = lambda i: (0, 0)
    row = lambda i: (i, 0)
    one = pl.Buffered(1)
    ga_blk = G_OFF // D
    return pl.pallas_call(
        _branch_merge_body,
        grid=(T // tm,),
        in_specs=[
            pl.BlockSpec((tm, D_CONV), row),
            pl.BlockSpec((tm, D_HYENA), row),
            pl.BlockSpec((tm, D), lambda i: (i, ga_blk)),
            pl.BlockSpec((tm, D), lambda i: (i, ga_blk + 1)),
            pl.BlockSpec((D_CONV, D), const, pipeline_mode=one),
            pl.BlockSpec((D_HYENA, D), const, pipeline_mode=one),
        ],
        out_specs=pl.BlockSpec((tm, D), row),
        out_shape=jax.ShapeDtypeStruct((T, D), jnp.bfloat16),
        compiler_params=_params("parallel"),
        name="branch_merge",
    )(ya, yb, proj, proj, wa, wh)


def _merge_body(m_ref, x_ref, wo_ref, gn_ref, wrt_ref, h_ref, xn_ref, aff_ref):
    h = x_ref[...] + _bdot(m_ref[...], wo_ref[...])
    h_ref[...] = h
    xn = _rms(h, gn_ref[...])
    _store_row_split(xn_ref, xn)
    lg = lax.dot_general(wrt_ref[...], xn, (((1,), (1,)), ((), ())),
                         preferred_element_type=jnp.float32, precision=_HP)
    ex = jnp.exp(lg - jnp.max(lg, axis=0, keepdims=True))
    aff_ref[...] = ex / jnp.sum(ex, axis=0, keepdims=True)


def _merge_out(merged, x, wo, g_ffn, w_router_t, tm=512):
    T, D = x.shape
    rpt = D // LANES
    const = lambda i: (0, 0)
    row = lambda i: (i, 0)
    one = pl.Buffered(1)
    return pl.pallas_call(
        _merge_body,
        grid=(T // tm,),
        in_specs=[
            pl.BlockSpec((tm, D), row),
            pl.BlockSpec((tm, D), row),
            pl.BlockSpec((D, D), const, pipeline_mode=one),
            pl.BlockSpec((1, D), const, pipeline_mode=one),
            pl.BlockSpec((N_EXPERTS, D), const, pipeline_mode=one),
        ],
        out_specs=[
            pl.BlockSpec((tm, D), row),
            pl.BlockSpec((tm * rpt, LANES), row),
            pl.BlockSpec((N_EXPERTS, tm), lambda i: (0, i)),
        ],
        out_shape=[
            jax.ShapeDtypeStruct((T, D), jnp.float32),
            jax.ShapeDtypeStruct((T * rpt, LANES), jnp.float32),
            jax.ShapeDtypeStruct((N_EXPERTS, T), jnp.float32),
        ],
        compiler_params=_params("parallel"),
        name="merge_out",
    )(merged, x, wo, g_ffn.reshape(1, D), w_router_t)


def _route_body(cap, aff_ref, idx_ref, g_ref, starts_ref):
    f32, bf16 = jnp.float32, jnp.bfloat16
    x = aff_ref[0]
    R = x.shape[0]
    bits = pltpu.bitcast(x, jnp.int32)

    def count(m):
        return jnp.sum(jnp.where(m, 1.0, 0.0), keepdims=True)

    def bit_step(i, cur):
        cand = cur | jnp.left_shift(jnp.int32(1), 30 - i)
        return jnp.where(count(bits >= cand) >= cap, cand, cur)

    thr = lax.fori_loop(0, 31, bit_step, jnp.zeros((1, 1), jnp.int32))

    def tri(n, cmp):
        a = lax.broadcasted_iota(jnp.int32, (n, n), 0)
        b = lax.broadcasted_iota(jnp.int32, (n, n), 1)
        return jnp.where(cmp(a, b), 1.0, 0.0).astype(bf16)

    upper = tri(LANES, lambda k, l: k <= l)
    ones = jnp.ones((LANES, LANES), bf16)
    strict_lower = tri(R, lambda r, k: k < r)

    def prefix(m01):
        mb = m01.astype(bf16)
        rowcum = _bdot(mb, upper)
        rowtot = _bdot(mb, ones)
        before = _bdot(strict_lower, rowtot.astype(bf16))
        return mb, rowcum, rowtot, before

    gt = jnp.where(bits > thr, 1.0, 0.0)
    eq = jnp.where(bits == thr, 1.0, 0.0)
    _, eq_cum, _, eq_before = prefix(eq)
    need = cap - jnp.sum(gt, keepdims=True)
    sel = gt + eq * jnp.where(eq_cum + eq_before <= need, 1.0, 0.0)
    selb, rowcum, rowtot, before = prefix(sel)

    through_row = (before + rowtot)[:, 0:1]
    j = lax.broadcasted_iota(jnp.int32, (1, cap), 1).astype(f32)
    done = through_row <= j
    r_j = jnp.sum(jnp.where(done, 1.0, 0.0), axis=0, keepdims=True)
    before_j = jnp.sum(jnp.where(done, rowtot[:, 0:1], 0.0), axis=0, keepdims=True)
    rows = lax.broadcasted_iota(jnp.int32, (R, cap), 0).astype(f32)
    pick_row = jnp.where(rows == r_j, 1.0, 0.0).astype(bf16)

    nt = (((1,), (1,)), ((), ()))
    lower = tri(LANES, lambda l, k: k <= l)
    rowcum_t = lax.dot_general(lower, selb, nt, preferred_element_type=f32)
    cum_j = _bdot(rowcum_t.astype(bf16), pick_row)
    lane_j = jnp.sum(jnp.where(cum_j <= j - before_j, 1.0, 0.0), axis=0, keepdims=True)
    idx_ref[0] = (r_j * LANES + lane_j).astype(jnp.int32)

    rowtot_t = lax.dot_general(jnp.ones((SUBLANES, LANES), bf16), selb, nt, preferred_element_type=f32)
    strict_upper = tri(R, lambda k, r: k < r)
    starts_ref[0] = _bdot(rowtot_t.astype(bf16), strict_upper).astype(jnp.int32)

    eye = tri(LANES, lambda a, b: a == b)
    hi = x.astype(bf16)
    r1 = x - hi.astype(f32)
    mid = r1.astype(bf16)
    lo = (r1 - mid.astype(f32)).astype(bf16)

    def pick(part):
        part_t = lax.dot_general(eye, part, nt, preferred_element_type=f32)
        return _bdot(part_t.astype(bf16), pick_row)

    aff_j = (pick(hi) + pick(mid)) + pick(lo)
    lanes = lax.broadcasted_iota(jnp.int32, (LANES, cap), 0).astype(f32)
    g_ref[0] = jnp.sum(jnp.where(lanes == lane_j, aff_j, 0.0), axis=0, keepdims=True)


def _route(aff_t, cap):
    E, T = aff_t.shape
    R = T // LANES
    return pl.pallas_call(
        functools.partial(_route_body, cap),
        grid=(E,),
        in_specs=[pl.BlockSpec((1, R, LANES), lambda e: (e, 0, 0))],
        out_specs=[pl.BlockSpec((1, 1, cap), lambda e: (e, 0, 0)),
                   pl.BlockSpec((1, 1, cap), lambda e: (e, 0, 0)),
                   pl.BlockSpec((1, SUBLANES, R), lambda e: (e, 0, 0))],
        out_shape=[jax.ShapeDtypeStruct((E, 1, cap), jnp.int32),
                   jax.ShapeDtypeStruct((E, 1, cap), jnp.float32),
                   jax.ShapeDtypeStruct((E, SUBLANES, R), jnp.int32)],
        compiler_params=_params("parallel"),
        name="route",
    )(aff_t.reshape(E, R, LANES))


_SEM_X, _SEM_OUT, _SEM_IDX = range(3)


def _moe_body(tm, rpt, n_tiles, n_f, idx_hbm, g_ref, wg_ref, wu_ref, wd_ref, xn_hbm, y_hbm,
              idx_smem, xrows, yrows, xb, acc, sem):
    n = pl.program_id(0) * pl.num_programs(1) + pl.program_id(1)
    f = pl.program_id(2)
    last = n_f - 1
    mid = min(2, last)

    def slab(ref, r):
        return ref.at[pl.ds(pl.multiple_of(r * rpt, rpt), rpt), :]

    def gather_x(tile):
        cp = pltpu.make_async_copy(idx_hbm.at[tile], idx_smem, sem.at[_SEM_IDX])
        cp.start()
        cp.wait()

        def body(j, c):
            pltpu.make_async_copy(slab(xn_hbm, idx_smem[j]), slab(xrows, j), sem.at[_SEM_X]).start()
            return c
        lax.fori_loop(0, tm, body, 0, unroll=8)

    x_landed = pltpu.make_async_copy(xn_hbm.at[pl.ds(0, tm * rpt), :], xrows, sem.at[_SEM_X])
    tile_rows = pl.ds(pl.multiple_of(n * (tm * rpt), tm * rpt), tm * rpt)
    y_out = pltpu.make_async_copy(yrows, y_hbm.at[tile_rows, :], sem.at[_SEM_OUT])

    @pl.when(f == 0)
    def _():
        @pl.when(n == 0)
        def _():
            gather_x(0)

        x_landed.wait()
        xb[...] = _load_row_split(xrows, tm, rpt).astype(jnp.bfloat16)
        acc[...] = jnp.zeros_like(acc)

    @pl.when((f == mid) & (n + 1 < n_tiles))
    def _():
        gather_x(n + 1)

    x = xb[...]
    a = _bdot(x, wg_ref[0].astype(jnp.bfloat16))
    b = _bdot(x, wu_ref[0].astype(jnp.bfloat16))
    hmid = (a * jax.nn.sigmoid(a) * b).astype(jnp.bfloat16)
    acc[...] += _bdot(hmid, wd_ref[0].astype(jnp.bfloat16))

    @pl.when(f == last)
    def _():
        @pl.when(n > 0)
        def _():
            y_out.wait()

        g = g_ref[0]
        for k in range(rpt):
            yrows[pl.ds(k, tm, stride=rpt), :] = acc[:, k * LANES:(k + 1) * LANES] * g
        y_out.start()

        @pl.when(n == n_tiles - 1)
        def _():
            y_out.wait()


def _moe(idx, g, xn_rs, w_gate, w_up, w_down, tf=256):
    E, _, C = idx.shape
    D, F = w_gate.shape[1], w_gate.shape[2]
    rpt = D // LANES
    tm = min(C, 1024)
    any_spec = pl.BlockSpec(memory_space=pl.ANY)
    n_tiles = E * (C // tm)
    return pl.pallas_call(
        functools.partial(_moe_body, tm, rpt, n_tiles, F // tf),
        grid=(E, C // tm, F // tf),
        in_specs=[
            any_spec,
            pl.BlockSpec((1, tm, 1), lambda e, i, f: (e, i, 0)),
            pl.BlockSpec((1, D, tf), lambda e, i, f: (e, 0, f)),
            pl.BlockSpec((1, D, tf), lambda e, i, f: (e, 0, f)),
            pl.BlockSpec((1, tf, D), lambda e, i, f: (e, f, 0)),
            any_spec,
        ],
        out_specs=any_spec,
        out_shape=jax.ShapeDtypeStruct((E * C * rpt, LANES), jnp.float32),
        scratch_shapes=[
            pltpu.SMEM((tm,), jnp.int32),
            pltpu.VMEM((tm * rpt, LANES), jnp.float32),
            pltpu.VMEM((tm * rpt, LANES), jnp.float32),
            pltpu.VMEM((tm, D), jnp.bfloat16),
            pltpu.VMEM((tm, D), jnp.float32),
            pltpu.SemaphoreType.DMA((3,)),
        ],
        compiler_params=pltpu.CompilerParams(
            dimension_semantics=("arbitrary", "arbitrary", "arbitrary"),
            vmem_limit_bytes=VMEM_LIMIT_BYTES, has_side_effects=True),
        name="moe",
    )(idx.reshape(n_tiles, tm), g.reshape(E, C, 1), w_gate, w_up, w_down, xn_rs)


_COMBINE_UNROLL = 4


def _combine_body(tt, rpt, cap, n_exp, n_rows, ch, idx_ref, starts_ref, h_ref, y_hbm, gn_ref, o_ref,
                  ybuf, acc, meta, sem):
    i = pl.program_id(0)
    t0 = i * tt
    total = n_exp * cap
    rows_per_tile = tt // LANES

    def lower_bound(e, row):
        s = starts_ref[e * n_rows + jnp.minimum(row, n_rows - 1)]
        return jnp.where(row < n_rows, s, cap)

    def chunk_copy(e, first_slot):
        start = jnp.minimum(first_slot, total - ch)
        src = y_hbm.at[pl.ds(pl.multiple_of(start * rpt, rpt), ch * rpt), :]
        return pltpu.make_async_copy(src, ybuf.at[e], sem.at[e]), first_slot - start

    for e in range(n_exp):
        a = lower_bound(e, i * rows_per_tile)
        b = lower_bound(e, (i + 1) * rows_per_tile)
        meta[0, e] = e * cap + a
        meta[1, e] = b - a
        chunk_copy(e, e * cap + a)[0].start()

    acc[...] = jnp.zeros_like(acc)

    def add_rows(e, first_slot, off, m):
        def token_rows(r):
            t = idx_ref[first_slot + r] - t0
            return pl.ds(pl.multiple_of(t * rpt, rpt), rpt)

        def y_rows(r):
            return pl.ds(pl.multiple_of((off + r) * rpt, rpt), rpt)

        def group(q, c):
            r0 = q * _COMBINE_UNROLL
            dst = [token_rows(r0 + u) for u in range(_COMBINE_UNROLL)]
            sums = [acc[dst[u], :] + ybuf[e, y_rows(r0 + u), :] for u in range(_COMBINE_UNROLL)]
            for u in range(_COMBINE_UNROLL):
                acc[dst[u], :] = sums[u]
            return c

        full = m // _COMBINE_UNROLL
        lax.fori_loop(0, full, group, 0)

        def single(r, c):
            d = token_rows(r)
            acc[d, :] = acc[d, :] + ybuf[e, y_rows(r), :]
            return c

        lax.fori_loop(full * _COMBINE_UNROLL, m, single, 0)

    for e in range(n_exp):
        first_slot = meta[0, e]
        count = meta[1, e]
        cp, off = chunk_copy(e, first_slot)
        cp.wait()
        m0 = jnp.minimum(count, ch - off)
        add_rows(e, first_slot, off, m0)

        def more(done):
            cp2, off2 = chunk_copy(e, first_slot + done)
            cp2.start()
            cp2.wait()
            m = jnp.minimum(count - done, ch - off2)
            add_rows(e, first_slot + done, off2, m)
            return done + m

        lax.while_loop(lambda done: done < count, more, m0)

    o_ref[...] = _rms(h_ref[...] + _load_row_split(acc, tt, rpt), gn_ref[...])


def _combine_norm(idx, starts, h, y_rs, g_final, tt=256, ch=64):
    E, _, C = idx.shape
    T, D = h.shape
    rpt = D // LANES
    n_rows = T // LANES
    grid_spec = pltpu.PrefetchScalarGridSpec(
        num_scalar_prefetch=2,
        grid=(T // tt,),
        in_specs=[
            pl.BlockSpec((tt, D), lambda i, *_: (i, 0)),
            pl.BlockSpec(memory_space=pl.ANY),
            pl.BlockSpec((1, D), lambda i, *_: (0, 0)),
        ],
        out_specs=pl.BlockSpec((tt, D), lambda i, *_: (i, 0)),
        scratch_shapes=[
            pltpu.VMEM((E, ch * rpt, LANES), jnp.float32),
            pltpu.VMEM((tt * rpt, LANES), jnp.float32),
            pltpu.SMEM((2, E), jnp.int32),
            pltpu.SemaphoreType.DMA((E,)),
        ],
    )
    return pl.pallas_call(
        functools.partial(_combine_body, tt, rpt, C, E, n_rows, ch),
        grid_spec=grid_spec,
        out_shape=jax.ShapeDtypeStruct((T, D), jnp.float32),
        compiler_params=_params("arbitrary"),
        name="combine_norm",
    )(idx.reshape(E * C), starts[:, 0, :].reshape(E * n_rows), h, y_rs, g_final.reshape(1, D))


def _trunk(x, p):
    B, L, D = x.shape
    T = B * L
    xf = x.reshape(T, D)
    proj = _norm_proj(xf, p["norm_mix"], p["w_in"])
    ya, x0, u, ub = _mix_front(proj, p["conv_a_w"], p["conv_h_w"], L)

    taps = _hyena_filter_taps(L, p)
    P, Q, KN = _filter_spectrum(L, taps)
    tc = 1024 if L <= 2048 else 512
    shp = (B, L, D_HYENA)
    yr, z2, yn = _fwd_dft(ub.reshape(shp), P, Q, KN, tf=512, tc=tc)
    yb = _inv_dft(yr, z2, yn, x0.reshape(shp), u.reshape(shp), p["hyena_skip"], tt=512, tc=tc)

    merged = _branch_merge(ya, yb.reshape(T, D_HYENA), proj, p["w_proj_a"], p["w_proj_h"])
    h, xn_rs, aff_t = _merge_out(merged, xf, p["w_out"], p["norm_ffn"], p["w_router"].T)

    cap = max(1, CAPACITY_FACTOR * T // N_EXPERTS)
    idx, g, starts = _route(aff_t, cap)
    y_rs = _moe(idx, g, xn_rs, p["w_gate"], p["w_up"], p["w_down"])
    return _combine_norm(idx, starts, h, y_rs, p["norm_final"]).reshape(B, L, D)


def kernel(x_prompt, x_sample, w_in, conv_a_w, conv_h_w, filt_w1, filt_b1, filt_w2, filt_b2, filt_w3, filt_b3, filt_w_out, filt_freq, hyena_skip, w_proj_a, w_proj_h, w_out, norm_mix, norm_ffn, w_router, w_gate, w_up, w_down, norm_final):
    bf16 = jnp.bfloat16
    p = dict(
        w_in=w_in[0].astype(bf16), conv_a_w=conv_a_w[0], conv_h_w=conv_h_w[0],
        filt_w1=filt_w1[0], filt_b1=filt_b1[0], filt_w2=filt_w2[0], filt_b2=filt_b2[0],
        filt_w3=filt_w3[0], filt_b3=filt_b3[0], filt_w_out=filt_w_out[0], filt_freq=filt_freq[0],
        hyena_skip=hyena_skip[0], w_proj_a=w_proj_a[0].astype(bf16), w_proj_h=w_proj_h[0].astype(bf16),
        w_out=w_out[0].astype(bf16), norm_mix=norm_mix[0], norm_ffn=norm_ffn[0],
        w_router=w_router[0], w_gate=w_gate[0], w_up=w_up[0], w_down=w_down[0],
        norm_final=norm_final,
    )
    return (_trunk(x_prompt, p), _trunk(x_sample, p))
```

```python
import functools
import math

import numpy as np

import jax
import jax.numpy as jnp
from jax import lax
from jax.experimental import pallas as pl
from jax.experimental.pallas import tpu as pltpu

D_MODEL = 2048
D_CONV = 1024
D_HYENA = 1024
SHORT_K = 3
FILTER_EMB = 33
FILTER_ORDER = 64
DECAY_FAST = 0.3
DECAY_SLOW = 1.5
DECAY_TARGET = 1e-2
N_EXPERTS = 16
CAPACITY_FACTOR = 2
D_EXPERT = 5632
EPS = 1e-6
D_IN = 3 * D_CONV + 3 * D_HYENA + 2 * D_MODEL
H_OFF = 3 * D_CONV
G_OFF = 3 * D_CONV + 3 * D_HYENA

LANES = 128
SUBLANES = 8
VMEM_LIMIT_BYTES = 56 * 1024 * 1024

_HP = lax.Precision.HIGHEST


def _rms(x, g):
    r = lax.rsqrt(jnp.mean(x * x, axis=-1, keepdims=True) + EPS)
    return (x * r) * g


def _bdot(a, b):
    return jnp.dot(a, b, preferred_element_type=jnp.float32)


def _params(*sem):
    return pltpu.CompilerParams(dimension_semantics=sem, vmem_limit_bytes=VMEM_LIMIT_BYTES)


def _norm_proj_body(x_ref, g_ref, w_ref, o_ref, xn_ref):
    @pl.when(pl.program_id(1) == 0)
    def _():
        xn_ref[...] = _rms(x_ref[...], g_ref[...]).astype(jnp.bfloat16)

    o_ref[...] = _bdot(xn_ref[...], w_ref[...])


def _norm_proj(x, g, w_bf16, tm=1024, tn=1024):
    T, D = x.shape
    N = w_bf16.shape[1]
    return pl.pallas_call(
        _norm_proj_body,
        grid=(T // tm, N // tn),
        in_specs=[
            pl.BlockSpec((tm, D), lambda i, j: (i, 0)),
            pl.BlockSpec((1, D), lambda i, j: (0, 0)),
            pl.BlockSpec((D, tn), lambda i, j: (0, j)),
        ],
        out_specs=pl.BlockSpec((tm, tn), lambda i, j: (i, j)),
        out_shape=jax.ShapeDtypeStruct((T, N), jnp.float32),
        scratch_shapes=[pltpu.VMEM((tm, D), jnp.bfloat16)],
        compiler_params=_params("parallel", "arbitrary"),
        name="norm_proj",
    )(x, g.reshape(1, D), w_bf16)


def _shift_down(x, first_row):
    rows = lax.broadcasted_iota(jnp.int32, x.shape, 0)
    return jnp.where(rows == 0, first_row, pltpu.roll(x, 1, axis=0))


def _shift_up(x, last_row):
    n = x.shape[0]
    rows = lax.broadcasted_iota(jnp.int32, x.shape, 0)
    return jnp.where(rows == n - 1, last_row, pltpu.roll(x, n - 1, axis=0))


def _mix_body(tiles_per_seq, p_ref, pp_ref, pn_ref, wa_ref, wh_ref,
              ya_ref, x0_ref, u_ref, ub_ref):
    i = pl.program_id(0)
    keep_prev = jnp.where(i % tiles_per_seq == 0, 0.0, 1.0)
    keep_next = jnp.where(i % tiles_per_seq == tiles_per_seq - 1, 0.0, 1.0)
    C = D_CONV
    q = p_ref[:, 0:C] * p_ref[:, 2 * C:3 * C]
    q_prev = pp_ref[SUBLANES - 1:SUBLANES, 0:C] * pp_ref[SUBLANES - 1:SUBLANES, 2 * C:3 * C] * keep_prev
    q_next = pn_ref[0:1, 0:C] * pn_ref[0:1, 2 * C:3 * C] * keep_next
    conv = (_shift_down(q, q_prev) * wa_ref[0:1, :] + q * wa_ref[1:2, :]
            + _shift_up(q, q_next) * wa_ref[2:3, :])
    ya_ref[...] = (p_ref[:, C:2 * C] * conv).astype(ya_ref.dtype)

    def hconv(j):
        lo, hi = H_OFF + j * D_HYENA, H_OFF + (j + 1) * D_HYENA
        wlo, whi = j * D_HYENA, (j + 1) * D_HYENA
        x = p_ref[:, lo:hi]
        xp = pp_ref[SUBLANES - 1:SUBLANES, lo:hi] * keep_prev
        xn = pn_ref[0:1, lo:hi] * keep_next
        return (_shift_down(x, xp) * wh_ref[0:1, wlo:whi] + x * wh_ref[1:2, wlo:whi]
                + _shift_up(x, xn) * wh_ref[2:3, wlo:whi])

    x0_ref[...] = hconv(0)
    u = hconv(2) * hconv(1)
    u_ref[...] = u
    ub_ref[...] = u.astype(ub_ref.dtype)


def _mix_front(proj, conv_a_w, conv_h_w, L, tl=256):
    T = proj.shape[0]
    W = G_OFF
    nb = tl // SUBLANES
    last_blk = T // SUBLANES - 1
    return pl.pallas_call(
        functools.partial(_mix_body, L // tl),
        grid=(T // tl,),
        in_specs=[
            pl.BlockSpec((tl, W), lambda i: (i, 0)),
            pl.BlockSpec((SUBLANES, W), lambda i: (jnp.maximum(i * nb - 1, 0), 0)),
            pl.BlockSpec((SUBLANES, W), lambda i: (jnp.minimum((i + 1) * nb, last_blk), 0)),
            pl.BlockSpec((SHORT_K, D_CONV), lambda i: (0, 0)),
            pl.BlockSpec((SHORT_K, 3 * D_HYENA), lambda i: (0, 0)),
        ],
        out_specs=[
            pl.BlockSpec((tl, D_CONV), lambda i: (i, 0)),
            pl.BlockSpec((tl, D_HYENA), lambda i: (i, 0)),
            pl.BlockSpec((tl, D_HYENA), lambda i: (i, 0)),
            pl.BlockSpec((tl, D_HYENA), lambda i: (i, 0)),
        ],
        out_shape=[
            jax.ShapeDtypeStruct((T, D_CONV), jnp.bfloat16),
            jax.ShapeDtypeStruct((T, D_HYENA), jnp.float32),
            jax.ShapeDtypeStruct((T, D_HYENA), jnp.float32),
            jax.ShapeDtypeStruct((T, D_HYENA), jnp.bfloat16),
        ],
        compiler_params=_params("parallel"),
        name="mix_front",
    )(proj, proj, proj, conv_a_w, conv_h_w)


@functools.lru_cache(maxsize=None)
def _dft_tables(L):
    m = (np.arange(L, dtype=np.int64)[:, None] * np.arange(L, dtype=np.int64)[None, :]) % (2 * L)
    ang = m.astype(np.float64) * (math.pi / L)
    c = np.cos(ang).astype(np.float32)
    s = np.sin(ang).astype(np.float32)
    alt = np.zeros((SUBLANES, L), np.float32)
    alt[0] = 1.0 - 2.0 * (np.arange(L) % 2)
    return c.astype(jnp.bfloat16), s.astype(jnp.bfloat16), alt.astype(jnp.bfloat16)


@functools.lru_cache(maxsize=None)
def _filter_consts(L):
    t = np.linspace(0.0, 1.0, L, dtype=np.float32)[:, None]
    bands = (FILTER_EMB - 1) // 2
    ang = (np.float32(2.0 * math.pi / L) * np.arange(L, dtype=np.float32)[:, None]
           * np.linspace(1e-4, bands - 1, bands, dtype=np.float32)[None, :])
    z = np.zeros((L, LANES), np.float32)
    z[:, 0:1] = t
    z[:, 1:1 + bands] = np.cos(ang)
    z[:, 1 + bands:1 + 2 * bands] = -np.sin(ang)
    max_decay = math.log(DECAY_TARGET) / DECAY_FAST
    min_decay = math.log(DECAY_TARGET) / DECAY_SLOW
    deltas = np.abs(np.linspace(min_decay, max_decay, D_HYENA, dtype=np.float32))[None, :]
    return z, deltas


def _filter_body(z_ref, dl_ref, w1_ref, b1_ref, w2_ref, b2_ref, w3_ref, b3_ref, wo_ref, fr_ref,
                 gs_hi_ref, gs_lo_ref, gd_hi_ref, gd_lo_ref):
    z = z_ref[...]
    fr = fr_ref[...]
    h = jnp.sin(fr * (jnp.dot(z, w1_ref[...], precision=_HP, preferred_element_type=jnp.float32) + b1_ref[...]))
    h = jnp.sin(fr * (jnp.dot(h, w2_ref[...], precision=_HP, preferred_element_type=jnp.float32) + b2_ref[...]))
    h = jnp.sin(fr * (jnp.dot(h, w3_ref[...], precision=_HP, preferred_element_type=jnp.float32) + b3_ref[...]))
    h = jnp.dot(h, wo_ref[...], precision=_HP, preferred_element_type=jnp.float32)
    decay = jnp.exp(-z[:, 0:1] * dl_ref[...])
    hf = h[:, :D_HYENA] * decay
    hb = h[:, D_HYENA:] * decay
    rows = lax.broadcasted_iota(jnp.int32, hb.shape, 0) + pl.program_id(0) * hb.shape[0]
    hb = jnp.where(rows == 0, 0.0, hb)
    gs = hf + hb
    gd = hb - hf
    gs_hi = gs.astype(jnp.bfloat16)
    gd_hi = gd.astype(jnp.bfloat16)
    gs_hi_ref[...] = gs_hi
    gd_hi_ref[...] = gd_hi
    gs_lo_ref[...] = (gs - gs_hi.astype(jnp.float32)).astype(jnp.bfloat16)
    gd_lo_ref[...] = (gd - gd_hi.astype(jnp.float32)).astype(jnp.bfloat16)


def _pad_to(a, shape):
    return jnp.pad(a, [(0, n - s) for s, n in zip(a.shape, shape)])


def _hyena_filter_taps(L, p, tl=512):
    z, deltas = _filter_consts(L)
    P = LANES
    w1 = _pad_to(p["filt_w1"], (P, P))
    w2 = _pad_to(p["filt_w2"], (P, P))
    w3 = _pad_to(p["filt_w3"], (P, P))
    wo = _pad_to(p["filt_w_out"], (P, 2 * D_HYENA))
    b1 = _pad_to(p["filt_b1"].reshape(1, -1), (1, P))
    b2 = _pad_to(p["filt_b2"].reshape(1, -1), (1, P))
    b3 = _pad_to(p["filt_b3"].reshape(1, -1), (1, P))
    fr = _pad_to(p["filt_freq"].reshape(1, -1), (1, P))
    const = lambda i: (0, 0)
    out = jax.ShapeDtypeStruct((L, D_HYENA), jnp.bfloat16)
    return pl.pallas_call(
        _filter_body,
        grid=(L // tl,),
        in_specs=[
            pl.BlockSpec((tl, P), lambda i: (i, 0)),
            pl.BlockSpec((1, D_HYENA), const),
            pl.BlockSpec((P, P), const), pl.BlockSpec((1, P), const),
            pl.BlockSpec((P, P), const), pl.BlockSpec((1, P), const),
            pl.BlockSpec((P, P), const), pl.BlockSpec((1, P), const),
            pl.BlockSpec((P, 2 * D_HYENA), const), pl.BlockSpec((1, P), const),
        ],
        out_specs=[pl.BlockSpec((tl, D_HYENA), lambda i: (i, 0))] * 4,
        out_shape=[out] * 4,
        compiler_params=_params("parallel"),
        name="hyena_filter",
    )(z, deltas, w1, b1, w2, b2, w3, b3, wo, fr)


def _spectrum_body(L, c_ref, s_ref, alt_ref, gsh_ref, gsl_ref, gdh_ref, gdl_ref,
                   p_ref, q_ref, kn_ref):
    c = c_ref[...]
    s = s_ref[...]
    tf = c.shape[0]
    f = lax.broadcasted_iota(jnp.int32, (tf, 1), 0) + pl.program_id(1) * tf
    scale = jnp.where(f == 0, 1.0, 2.0) * (1.0 / (2 * L))
    p_ref[...] = (_bdot(c, gsh_ref[...]) + _bdot(c, gsl_ref[...])) * scale
    q_ref[...] = (_bdot(s, gdh_ref[...]) + _bdot(s, gdl_ref[...])) * scale
    alt = alt_ref[...]
    kn_ref[...] = (_bdot(alt, gsh_ref[...]) + _bdot(alt, gsl_ref[...])) * (1.0 / (2 * L))


def _filter_spectrum(L, taps, tf=512, tc=256):
    cmat, smat, alt = _dft_tables(L)
    gcol = pl.BlockSpec((L, tc), lambda j, k: (0, j))
    return pl.pallas_call(
        functools.partial(_spectrum_body, L),
        grid=(D_HYENA // tc, L // tf),
        in_specs=[
            pl.BlockSpec((tf, L), lambda j, k: (k, 0)),
            pl.BlockSpec((tf, L), lambda j, k: (k, 0)),
            pl.BlockSpec((SUBLANES, L), lambda j, k: (0, 0)),
            gcol, gcol, gcol, gcol,
        ],
        out_specs=[
            pl.BlockSpec((tf, tc), lambda j, k: (k, j)),
            pl.BlockSpec((tf, tc), lambda j, k: (k, j)),
            pl.BlockSpec((SUBLANES, tc), lambda j, k: (0, j)),
        ],
        out_shape=[
            jax.ShapeDtypeStruct((L, D_HYENA), jnp.float32),
            jax.ShapeDtypeStruct((L, D_HYENA), jnp.float32),
            jax.ShapeDtypeStruct((SUBLANES, D_HYENA), jnp.float32),
        ],
        compiler_params=_params("parallel", "arbitrary"),
        name="filter_spectrum",
    )(cmat, smat, alt, *taps)


def _fwd_dft_body(c_ref, s_ref, alt_ref, u_ref, p_ref, q_ref, kn_ref, yr_ref, z2_ref, yn_ref):
    u = u_ref[0]
    a = _bdot(c_ref[...], u)
    b = _bdot(s_ref[...], u)
    p = p_ref[...]
    q = q_ref[...]
    yr_ref[0] = (a * p + b * q).astype(yr_ref.dtype)
    z2_ref[0] = (b * p - a * q).astype(z2_ref.dtype)
    yn_ref[0] = _bdot(alt_ref[...], u) * kn_ref[...]


def _fwd_dft(u_bf16, P, Q, KN, tf, tc):
    B, L, C = u_bf16.shape
    cmat, smat, alt = _dft_tables(L)
    return pl.pallas_call(
        _fwd_dft_body,
        grid=(B, C // tc, L // tf),
        in_specs=[
            pl.BlockSpec((tf, L), lambda b, j, k: (k, 0)),
            pl.BlockSpec((tf, L), lambda b, j, k: (k, 0)),
            pl.BlockSpec((SUBLANES, L), lambda b, j, k: (0, 0)),
            pl.BlockSpec((1, L, tc), lambda b, j, k: (b, 0, j)),
            pl.BlockSpec((tf, tc), lambda b, j, k: (k, j)),
            pl.BlockSpec((tf, tc), lambda b, j, k: (k, j)),
            pl.BlockSpec((SUBLANES, tc), lambda b, j, k: (0, j)),
        ],
        out_specs=[
            pl.BlockSpec((1, tf, tc), lambda b, j, k: (b, k, j)),
            pl.BlockSpec((1, tf, tc), lambda b, j, k: (b, k, j)),
            pl.BlockSpec((1, SUBLANES, tc), lambda b, j, k: (b, 0, j)),
        ],
        out_shape=[
            jax.ShapeDtypeStruct((B, L, C), jnp.bfloat16),
            jax.ShapeDtypeStruct((B, L, C), jnp.bfloat16),
            jax.ShapeDtypeStruct((B, SUBLANES, C), jnp.float32),
        ],
        compiler_params=_params("parallel", "parallel", "arbitrary"),
        name="fwd_dft",
    )(cmat, smat, alt, u_bf16, P, Q, KN)


def _inv_dft_body(c_ref, s_ref, yr_ref, z2_ref, yn_ref, x0_ref, u_ref, skip_ref, o_ref):
    y = _bdot(c_ref[...], yr_ref[0]) + _bdot(s_ref[...], z2_ref[0])
    tt = y.shape[0]
    t = lax.broadcasted_iota(jnp.int32, (tt, 1), 0) + pl.program_id(2) * tt
    sign = (1 - 2 * (t & 1)).astype(jnp.float32)
    y = y + sign * yn_ref[0, 0:1, :]
    o_ref[0] = (x0_ref[0] * (y + u_ref[0] * skip_ref[...])).astype(o_ref.dtype)


def _inv_dft(yr, z2, yn, x0, u, skip, tt, tc):
    B, L, C = yr.shape
    cmat, smat, _ = _dft_tables(L)
    return pl.pallas_call(
        _inv_dft_body,
        grid=(B, C // tc, L // tt),
        in_specs=[
            pl.BlockSpec((tt, L), lambda b, j, k: (k, 0)),
            pl.BlockSpec((tt, L), lambda b, j, k: (k, 0)),
            pl.BlockSpec((1, L, tc), lambda b, j, k: (b, 0, j)),
            pl.BlockSpec((1, L, tc), lambda b, j, k: (b, 0, j)),
            pl.BlockSpec((1, SUBLANES, tc), lambda b, j, k: (b, 0, j)),
            pl.BlockSpec((1, tt, tc), lambda b, j, k: (b, k, j)),
            pl.BlockSpec((1, tt, tc), lambda b, j, k: (b, k, j)),
            pl.BlockSpec((1, tc), lambda b, j, k: (0, j)),
        ],
        out_specs=pl.BlockSpec((1, tt, tc), lambda b, j, k: (b, k, j)),
        out_shape=jax.ShapeDtypeStruct((B, L, C), jnp.bfloat16),
        compiler_params=_params("parallel", "parallel", "arbitrary"),
        name="inv_dft",
    )(cmat, smat, yr, z2, yn, x0, u, skip.reshape(1, C))


ROW_SPLIT_WIDTH = LANES


def _store_row_split(ref, x):
    w = ref.shape[1]
    rpt = x.shape[1] // w
    n = x.shape[0]
    for k in range(rpt):
        ref[pl.ds(k, n, stride=rpt), :] = x[:, k * w:(k + 1) * w]


def _load_row_split(ref, n, rpt):
    return jnp.concatenate([ref[pl.ds(k, n, stride=rpt), :] for k in range(rpt)], axis=1)


def _branch_merge_body(ya_ref, yb_ref, ga_ref, gb_ref, wa_ref, wh_ref, m_ref):
    pa = _bdot(ya_ref[...], wa_ref[...])
    ph = _bdot(yb_ref[...], wh_ref[...])
    merged = jax.nn.sigmoid(ga_ref[...]) * pa + jax.nn.sigmoid(gb_ref[...]) * ph
    m_ref[...] = merged.astype(m_ref.dtype)


def _branch_merge(ya, yb, proj, wa, wh, tm=512):
    T = ya.shape[0]
    D = wa.shape[1]
    const = lambda i: (0, 0)
    row = lambda i: (i, 0)
    one = pl.Buffered(1)
    ga_blk = G_OFF // D
    return pl.pallas_call(
        _branch_merge_body,
        grid=(T // tm,),
        in_specs=[
            pl.BlockSpec((tm, D_CONV), row),
            pl.BlockSpec((tm, D_HYENA), row),
            pl.BlockSpec((tm, D), lambda i: (i, ga_blk)),
            pl.BlockSpec((tm, D), lambda i: (i, ga_blk + 1)),
            pl.BlockSpec((D_CONV, D), const, pipeline_mode=one),
            pl.BlockSpec((D_HYENA, D), const, pipeline_mode=one),
        ],
        out_specs=pl.BlockSpec((tm, D), row),
        out_shape=jax.ShapeDtypeStruct((T, D), jnp.bfloat16),
        compiler_params=_params("parallel"),
        name="branch_merge",
    )(ya, yb, proj, proj, wa, wh)


def _merge_body(m_ref, x_ref, wo_ref, gn_ref, wrt_ref, h_ref, xn_ref, aff_ref):
    h = x_ref[...] + _bdot(m_ref[...], wo_ref[...])
    h_ref[...] = h
    xn = _rms(h, gn_ref[...])
    _store_row_split(xn_ref, xn)
    lg = lax.dot_general(wrt_ref[...], xn, (((1,), (1,)), ((), ())),
                         preferred_element_type=jnp.float32, precision=_HP)
    ex = jnp.exp(lg - jnp.max(lg, axis=0, keepdims=True))
    aff_ref[...] = ex / jnp.sum(ex, axis=0, keepdims=True)


def _merge_out(merged, x, wo, g_ffn, w_router_t, tm=512):
    T, D = x.shape
    rpt = D // ROW_SPLIT_WIDTH
    const = lambda i: (0, 0)
    row = lambda i: (i, 0)
    one = pl.Buffered(1)
    return pl.pallas_call(
        _merge_body,
        grid=(T // tm,),
        in_specs=[
            pl.BlockSpec((tm, D), row),
            pl.BlockSpec((tm, D), row),
            pl.BlockSpec((D, D), const, pipeline_mode=one),
            pl.BlockSpec((1, D), const, pipeline_mode=one),
            pl.BlockSpec((N_EXPERTS, D), const, pipeline_mode=one),
        ],
        out_specs=[
            pl.BlockSpec((tm, D), row),
            pl.BlockSpec((tm * rpt, ROW_SPLIT_WIDTH), row),
            pl.BlockSpec((N_EXPERTS, tm), lambda i: (0, i)),
        ],
        out_shape=[
            jax.ShapeDtypeStruct((T, D), jnp.float32),
            jax.ShapeDtypeStruct((T * rpt, ROW_SPLIT_WIDTH), jnp.float32),
            jax.ShapeDtypeStruct((N_EXPERTS, T), jnp.float32),
        ],
        compiler_params=_params("parallel"),
        name="merge_out",
    )(merged, x, wo, g_ffn.reshape(1, D), w_router_t)


def _route_body(cap, aff_ref, idx_ref, g_ref, starts_ref):
    f32, bf16 = jnp.float32, jnp.bfloat16
    x = aff_ref[0]
    R = x.shape[0]
    bits = pltpu.bitcast(x, jnp.int32)

    def count(m):
        return jnp.sum(jnp.where(m, 1.0, 0.0), keepdims=True)

    def bit_step(i, cur):
        cand = cur | jnp.left_shift(jnp.int32(1), 30 - i)
        return jnp.where(count(bits >= cand) >= cap, cand, cur)

    thr = lax.fori_loop(0, 31, bit_step, jnp.zeros((1, 1), jnp.int32))

    def tri(n, cmp):
        a = lax.broadcasted_iota(jnp.int32, (n, n), 0)
        b = lax.broadcasted_iota(jnp.int32, (n, n), 1)
        return jnp.where(cmp(a, b), 1.0, 0.0).astype(bf16)

    upper = tri(LANES, lambda k, l: k <= l)
    ones = jnp.ones((LANES, LANES), bf16)
    strict_lower = tri(R, lambda r, k: k < r)

    def prefix(m01):
        mb = m01.astype(bf16)
        rowcum = _bdot(mb, upper)
        rowtot = _bdot(mb, ones)
        before = _bdot(strict_lower, rowtot.astype(bf16))
        return mb, rowcum, rowtot, before

    gt = jnp.where(bits > thr, 1.0, 0.0)
    eq = jnp.where(bits == thr, 1.0, 0.0)
    _, eq_cum, _, eq_before = prefix(eq)
    need = cap - jnp.sum(gt, keepdims=True)
    sel = gt + eq * jnp.where(eq_cum + eq_before <= need, 1.0, 0.0)
    selb, rowcum, rowtot, before = prefix(sel)

    through_row = (before + rowtot)[:, 0:1]
    j = lax.broadcasted_iota(jnp.int32, (1, cap), 1).astype(f32)
    done = through_row <= j
    r_j = jnp.sum(jnp.where(done, 1.0, 0.0), axis=0, keepdims=True)
    before_j = jnp.sum(jnp.where(done, rowtot[:, 0:1], 0.0), axis=0, keepdims=True)
    rows = lax.broadcasted_iota(jnp.int32, (R, cap), 0).astype(f32)
    pick_row = jnp.where(rows == r_j, 1.0, 0.0).astype(bf16)

    nt = (((1,), (1,)), ((), ()))
    lower = tri(LANES, lambda l, k: k <= l)
    rowcum_t = lax.dot_general(lower, selb, nt, preferred_element_type=f32)
    cum_j = _bdot(rowcum_t.astype(bf16), pick_row)
    lane_j = jnp.sum(jnp.where(cum_j <= j - before_j, 1.0, 0.0), axis=0, keepdims=True)
    idx_ref[0] = (r_j * LANES + lane_j).astype(jnp.int32)

    rowtot_t = lax.dot_general(jnp.ones((SUBLANES, LANES), bf16), selb, nt, preferred_element_type=f32)
    strict_upper = tri(R, lambda k, r: k < r)
    starts_ref[0] = _bdot(rowtot_t.astype(bf16), strict_upper).astype(jnp.int32)

    eye = tri(LANES, lambda a, b: a == b)
    hi = x.astype(bf16)
    r1 = x - hi.astype(f32)
    mid = r1.astype(bf16)
    lo = (r1 - mid.astype(f32)).astype(bf16)

    def pick(part):
        part_t = lax.dot_general(eye, part, nt, preferred_element_type=f32)
        return _bdot(part_t.astype(bf16), pick_row)

    aff_j = (pick(hi) + pick(mid)) + pick(lo)
    lanes = lax.broadcasted_iota(jnp.int32, (LANES, cap), 0).astype(f32)
    g_ref[0] = jnp.sum(jnp.where(lanes == lane_j, aff_j, 0.0), axis=0, keepdims=True)


def _route(aff_t, cap):
    E, T = aff_t.shape
    R = T // LANES
    return pl.pallas_call(
        functools.partial(_route_body, cap),
        grid=(E,),
        in_specs=[pl.BlockSpec((1, R, LANES), lambda e: (e, 0, 0))],
        out_specs=[pl.BlockSpec((1, 1, cap), lambda e: (e, 0, 0)),
                   pl.BlockSpec((1, 1, cap), lambda e: (e, 0, 0)),
                   pl.BlockSpec((1, SUBLANES, R), lambda e: (e, 0, 0))],
        out_shape=[jax.ShapeDtypeStruct((E, 1, cap), jnp.int32),
                   jax.ShapeDtypeStruct((E, 1, cap), jnp.float32),
                   jax.ShapeDtypeStruct((E, SUBLANES, R), jnp.int32)],
        compiler_params=_params("parallel"),
        name="route",
    )(aff_t.reshape(E, R, LANES))


_SEM_X, _SEM_OUT, _SEM_IDX = range(3)


def _moe_body(tm, rpt, n_tiles, n_f, idx_hbm, g_ref, wg_ref, wu_ref, wd_ref, xn_hbm, y_hbm,
              idx_smem, xrows, yrows, xb, acc, sem):
    n = pl.program_id(0) * pl.num_programs(1) + pl.program_id(1)
    f = pl.program_id(2)
    last = n_f - 1
    mid = min(2, last)

    def slab(ref, r):
        return ref.at[pl.ds(pl.multiple_of(r * rpt, rpt), rpt), :]

    def gather_x(tile):
        cp = pltpu.make_async_copy(idx_hbm.at[tile], idx_smem, sem.at[_SEM_IDX])
        cp.start()
        cp.wait()

        def body(j, c):
            pltpu.make_async_copy(slab(xn_hbm, idx_smem[j]), slab(xrows, j), sem.at[_SEM_X]).start()
            return c
        lax.fori_loop(0, tm, body, 0, unroll=8)

    x_landed = pltpu.make_async_copy(xn_hbm.at[pl.ds(0, tm * rpt), :], xrows, sem.at[_SEM_X])
    tile_rows = pl.ds(pl.multiple_of(n * (tm * rpt), tm * rpt), tm * rpt)
    y_out = pltpu.make_async_copy(yrows, y_hbm.at[tile_rows, :], sem.at[_SEM_OUT])

    @pl.when(f == 0)
    def _():
        @pl.when(n == 0)
        def _():
            gather_x(0)

        x_landed.wait()
        xb[...] = _load_row_split(xrows, tm, rpt).astype(jnp.bfloat16)
        acc[...] = jnp.zeros_like(acc)

    @pl.when((f == mid) & (n + 1 < n_tiles))
    def _():
        gather_x(n + 1)

    x = xb[...]
    a = _bdot(x, wg_ref[0].astype(jnp.bfloat16))
    b = _bdot(x, wu_ref[0].astype(jnp.bfloat16))
    hmid = (a * jax.nn.sigmoid(a) * b).astype(jnp.bfloat16)
    acc[...] += _bdot(hmid, wd_ref[0].astype(jnp.bfloat16))

    @pl.when(f == last)
    def _():
        @pl.when(n > 0)
        def _():
            y_out.wait()

        g = g_ref[0]
        w = yrows.shape[1]
        for k in range(rpt):
            yrows[pl.ds(k, tm, stride=rpt), :] = acc[:, k * w:(k + 1) * w] * g
        y_out.start()

        @pl.when(n == n_tiles - 1)
        def _():
            y_out.wait()


def _moe(idx, g, xn_rs, w_gate, w_up, w_down, tf=256):
    E, _, C = idx.shape
    D, F = w_gate.shape[1], w_gate.shape[2]
    rpt = D // ROW_SPLIT_WIDTH
    tm = min(C, 1024)
    any_spec = pl.BlockSpec(memory_space=pl.ANY)
    n_tiles = E * (C // tm)
    return pl.pallas_call(
        functools.partial(_moe_body, tm, rpt, n_tiles, F // tf),
        grid=(E, C // tm, F // tf),
        in_specs=[
            any_spec,
            pl.BlockSpec((1, tm, 1), lambda e, i, f: (e, i, 0)),
            pl.BlockSpec((1, D, tf), lambda e, i, f: (e, 0, f)),
            pl.BlockSpec((1, D, tf), lambda e, i, f: (e, 0, f)),
            pl.BlockSpec((1, tf, D), lambda e, i, f: (e, f, 0)),
            any_spec,
        ],
        out_specs=any_spec,
        out_shape=jax.ShapeDtypeStruct((E * C * rpt, ROW_SPLIT_WIDTH), jnp.float32),
        scratch_shapes=[
            pltpu.SMEM((tm,), jnp.int32),
            pltpu.VMEM((tm * rpt, ROW_SPLIT_WIDTH), jnp.float32),
            pltpu.VMEM((tm * rpt, ROW_SPLIT_WIDTH), jnp.float32),
            pltpu.VMEM((tm, D), jnp.bfloat16),
            pltpu.VMEM((tm, D), jnp.float32),
            pltpu.SemaphoreType.DMA((3,)),
        ],
        compiler_params=pltpu.CompilerParams(
            dimension_semantics=("arbitrary", "arbitrary", "arbitrary"),
            vmem_limit_bytes=VMEM_LIMIT_BYTES, has_side_effects=True),
        name="moe",
    )(idx.reshape(n_tiles, tm), g.reshape(E, C, 1), w_gate, w_up, w_down, xn_rs)


_COMBINE_UNROLL = 4


def _combine_body(tt, rpt, cap, n_exp, n_rows, ch, idx_ref, starts_ref, h_ref, y_hbm, gn_ref, o_ref,
                  ybuf, acc, sem):
    i = pl.program_id(0)
    slot = i % 2
    t0 = i * tt
    total = n_exp * cap
    rows_per_tile = tt // LANES

    def lower_bound(e, row):
        s = starts_ref[e * n_rows + jnp.minimum(row, n_rows - 1)]
        return jnp.where(row < n_rows, s, cap)

    def run_of(e, tile):
        a = lower_bound(e, tile * rows_per_tile)
        b = lower_bound(e, (tile + 1) * rows_per_tile)
        return e * cap + a, b - a

    def chunk_copy(e, first_slot, s):
        start = jnp.minimum(first_slot, total - ch)
        src = y_hbm.at[pl.ds(pl.multiple_of(start * rpt, rpt), ch * rpt), :]
        return pltpu.make_async_copy(src, ybuf.at[s, e], sem.at[s, e]), first_slot - start

    def fetch_first_chunks(tile, s):
        for e in range(n_exp):
            chunk_copy(e, run_of(e, tile)[0], s)[0].start()

    @pl.when(i == 0)
    def _():
        fetch_first_chunks(0, 0)

    @pl.when(i + 1 < pl.num_programs(0))
    def _():
        fetch_first_chunks(i + 1, 1 - slot)

    acc[...] = jnp.zeros_like(acc)

    def add_rows(e, first_slot, off, m):
        def token_rows(r):
            t = idx_ref[first_slot + r] - t0
            return pl.ds(pl.multiple_of(t * rpt, rpt), rpt)

        def y_rows(r):
            return pl.ds(pl.multiple_of((off + r) * rpt, rpt), rpt)

        def group(q, c):
            r0 = q * _COMBINE_UNROLL
            dst = [token_rows(r0 + u) for u in range(_COMBINE_UNROLL)]
            sums = [acc[dst[u], :] + ybuf[slot, e, y_rows(r0 + u), :] for u in range(_COMBINE_UNROLL)]
            for u in range(_COMBINE_UNROLL):
                acc[dst[u], :] = sums[u]
            return c

        full = m // _COMBINE_UNROLL
        lax.fori_loop(0, full, group, 0)

        def single(r, c):
            d = token_rows(r)
            acc[d, :] = acc[d, :] + ybuf[slot, e, y_rows(r), :]
            return c

        lax.fori_loop(full * _COMBINE_UNROLL, m, single, 0)

    for e in range(n_exp):
        first_slot, count = run_of(e, i)
        cp, off = chunk_copy(e, first_slot, slot)
        cp.wait()
        m0 = jnp.minimum(count, ch - off)
        add_rows(e, first_slot, off, m0)

        def more(done):
            cp2, off2 = chunk_copy(e, first_slot + done, slot)
            cp2.start()
            cp2.wait()
            m = jnp.minimum(count - done, ch - off2)
            add_rows(e, first_slot + done, off2, m)
            return done + m

        lax.while_loop(lambda done: done < count, more, m0)

    o_ref[...] = _rms(h_ref[...] + _load_row_split(acc, tt, rpt), gn_ref[...])


def _combine_norm(idx, starts, h, y_rs, g_final, tt=256, ch=64):
    E, _, C = idx.shape
    T, D = h.shape
    rpt = D // ROW_SPLIT_WIDTH
    n_rows = T // LANES
    grid_spec = pltpu.PrefetchScalarGridSpec(
        num_scalar_prefetch=2,
        grid=(T // tt,),
        in_specs=[
            pl.BlockSpec((tt, D), lambda i, *_: (i, 0)),
            pl.BlockSpec(memory_space=pl.ANY),
            pl.BlockSpec((1, D), lambda i, *_: (0, 0)),
        ],
        out_specs=pl.BlockSpec((tt, D), lambda i, *_: (i, 0)),
        scratch_shapes=[
            pltpu.VMEM((2, E, ch * rpt, ROW_SPLIT_WIDTH), jnp.float32),
            pltpu.VMEM((tt * rpt, ROW_SPLIT_WIDTH), jnp.float32),
            pltpu.SemaphoreType.DMA((2, E)),
        ],
    )
    return pl.pallas_call(
        functools.partial(_combine_body, tt, rpt, C, E, n_rows, ch),
        grid_spec=grid_spec,
        out_shape=jax.ShapeDtypeStruct((T, D), jnp.float32),
        compiler_params=_params("arbitrary"),
        name="combine_norm",
    )(idx.reshape(E * C), starts[:, 0, :].reshape(E * n_rows), h, y_rs, g_final.reshape(1, D))


def _trunk(x, p):
    B, L, D = x.shape
    T = B * L
    xf = x.reshape(T, D)
    proj = _norm_proj(xf, p["norm_mix"], p["w_in"])
    ya, x0, u, ub = _mix_front(proj, p["conv_a_w"], p["conv_h_w"], L)

    taps = _hyena_filter_taps(L, p)
    P, Q, KN = _filter_spectrum(L, taps)
    tc = 1024 if L <= 2048 else 512
    shp = (B, L, D_HYENA)
    yr, z2, yn = _fwd_dft(ub.reshape(shp), P, Q, KN, tf=512, tc=tc)
    yb = _inv_dft(yr, z2, yn, x0.reshape(shp), u.reshape(shp), p["hyena_skip"], tt=512, tc=tc)

    merged = _branch_merge(ya, yb.reshape(T, D_HYENA), proj, p["w_proj_a"], p["w_proj_h"])
    h, xn_rs, aff_t = _merge_out(merged, xf, p["w_out"], p["norm_ffn"], p["w_router"].T)

    cap = max(1, CAPACITY_FACTOR * T // N_EXPERTS)
    idx, g, starts = _route(aff_t, cap)
    y_rs = _moe(idx, g, xn_rs, p["w_gate"], p["w_up"], p["w_down"])
    return _combine_norm(idx, starts, h, y_rs, p["norm_final"]).reshape(B, L, D)


def kernel(x_prompt, x_sample, w_in, conv_a_w, conv_h_w, filt_w1, filt_b1, filt_w2, filt_b2, filt_w3, filt_b3, filt_w_out, filt_freq, hyena_skip, w_proj_a, w_proj_h, w_out, norm_mix, norm_ffn, w_router, w_gate, w_up, w_down, norm_final):
    bf16 = jnp.bfloat16
    p = dict(
        w_in=w_in[0].astype(bf16), conv_a_w=conv_a_w[0], conv_h_w=conv_h_w[0],
        filt_w1=filt_w1[0], filt_b1=filt_b1[0], filt_w2=filt_w2[0], filt_b2=filt_b2[0],
        filt_w3=filt_w3[0], filt_b3=filt_b3[0], filt_w_out=filt_w_out[0], filt_freq=filt_freq[0],
        hyena_skip=hyena_skip[0], w_proj_a=w_proj_a[0].astype(bf16), w_proj_h=w_proj_h[0].astype(bf16),
        w_out=w_out[0].astype(bf16), norm_mix=norm_mix[0], norm_ffn=norm_ffn[0],
        w_router=w_router[0], w_gate=w_gate[0], w_up=w_up[0], w_down=w_down[0],
        norm_final=norm_final,
    )
    return (_trunk(x_prompt, p), _trunk(x_sample, p))
```

```python
import functools
import math

import numpy as np

import jax
import jax.numpy as jnp
from jax import lax
from jax.experimental import pallas as pl
from jax.experimental.pallas import tpu as pltpu

D_MODEL = 2048
D_CONV = 1024
D_HYENA = 1024
SHORT_K = 3
FILTER_EMB = 33
FILTER_ORDER = 64
DECAY_FAST = 0.3
DECAY_SLOW = 1.5
DECAY_TARGET = 1e-2
N_EXPERTS = 16
CAPACITY_FACTOR = 2
D_EXPERT = 5632
EPS = 1e-6
D_IN = 3 * D_CONV + 3 * D_HYENA + 2 * D_MODEL
H_OFF = 3 * D_CONV
G_OFF = 3 * D_CONV + 3 * D_HYENA

LANES = 128
SUBLANES = 8
VMEM_LIMIT_BYTES = 56 * 1024 * 1024

_HP = lax.Precision.HIGHEST


def _rms(x, g):
    r = lax.rsqrt(jnp.mean(x * x, axis=-1, keepdims=True) + EPS)
    return (x * r) * g


def _bdot(a, b):
    return jnp.dot(a, b, preferred_element_type=jnp.float32)


def _params(*sem):
    return pltpu.CompilerParams(dimension_semantics=sem, vmem_limit_bytes=VMEM_LIMIT_BYTES)


def _norm_proj_body(x_ref, g_ref, w_ref, o_ref, xn_ref):
    @pl.when(pl.program_id(1) == 0)
    def _():
        xn_ref[...] = _rms(x_ref[...], g_ref[...]).astype(jnp.bfloat16)

    o_ref[...] = _bdot(xn_ref[...], w_ref[...])


def _norm_proj(x, g, w_bf16, tm=1024, tn=1024):
    T, D = x.shape
    N = w_bf16.shape[1]
    return pl.pallas_call(
        _norm_proj_body,
        grid=(T // tm, N // tn),
        in_specs=[
            pl.BlockSpec((tm, D), lambda i, j: (i, 0)),
            pl.BlockSpec((1, D), lambda i, j: (0, 0)),
            pl.BlockSpec((D, tn), lambda i, j: (0, j)),
        ],
        out_specs=pl.BlockSpec((tm, tn), lambda i, j: (i, j)),
        out_shape=jax.ShapeDtypeStruct((T, N), jnp.float32),
        scratch_shapes=[pltpu.VMEM((tm, D), jnp.bfloat16)],
        compiler_params=_params("parallel", "arbitrary"),
        name="norm_proj",
    )(x, g.reshape(1, D), w_bf16)


def _shift_down(x, first_row):
    rows = lax.broadcasted_iota(jnp.int32, x.shape, 0)
    return jnp.where(rows == 0, first_row, pltpu.roll(x, 1, axis=0))


def _shift_up(x, last_row):
    n = x.shape[0]
    rows = lax.broadcasted_iota(jnp.int32, x.shape, 0)
    return jnp.where(rows == n - 1, last_row, pltpu.roll(x, n - 1, axis=0))


def _mix_body(tiles_per_seq, p_ref, pp_ref, pn_ref, wa_ref, wh_ref,
              ya_ref, x0_ref, u_ref, ub_ref):
    i = pl.program_id(0)
    keep_prev = jnp.where(i % tiles_per_seq == 0, 0.0, 1.0)
    keep_next = jnp.where(i % tiles_per_seq == tiles_per_seq - 1, 0.0, 1.0)
    C = D_CONV
    q = p_ref[:, 0:C] * p_ref[:, 2 * C:3 * C]
    q_prev = pp_ref[SUBLANES - 1:SUBLANES, 0:C] * pp_ref[SUBLANES - 1:SUBLANES, 2 * C:3 * C] * keep_prev
    q_next = pn_ref[0:1, 0:C] * pn_ref[0:1, 2 * C:3 * C] * keep_next
    conv = (_shift_down(q, q_prev) * wa_ref[0:1, :] + q * wa_ref[1:2, :]
            + _shift_up(q, q_next) * wa_ref[2:3, :])
    ya_ref[...] = (p_ref[:, C:2 * C] * conv).astype(ya_ref.dtype)

    def hconv(j):
        lo, hi = H_OFF + j * D_HYENA, H_OFF + (j + 1) * D_HYENA
        wlo, whi = j * D_HYENA, (j + 1) * D_HYENA
        x = p_ref[:, lo:hi]
        xp = pp_ref[SUBLANES - 1:SUBLANES, lo:hi] * keep_prev
        xn = pn_ref[0:1, lo:hi] * keep_next
        return (_shift_down(x, xp) * wh_ref[0:1, wlo:whi] + x * wh_ref[1:2, wlo:whi]
                + _shift_up(x, xn) * wh_ref[2:3, wlo:whi])

    x0_ref[...] = hconv(0)
    u = hconv(2) * hconv(1)
    u_ref[...] = u
    ub_ref[...] = u.astype(ub_ref.dtype)


def _mix_front(proj, conv_a_w, conv_h_w, L, tl=256):
    T = proj.shape[0]
    W = G_OFF
    nb = tl // SUBLANES
    last_blk = T // SUBLANES - 1
    return pl.pallas_call(
        functools.partial(_mix_body, L // tl),
        grid=(T // tl,),
        in_specs=[
            pl.BlockSpec((tl, W), lambda i: (i, 0)),
            pl.BlockSpec((SUBLANES, W), lambda i: (jnp.maximum(i * nb - 1, 0), 0)),
            pl.BlockSpec((SUBLANES, W), lambda i: (jnp.minimum((i + 1) * nb, last_blk), 0)),
            pl.BlockSpec((SHORT_K, D_CONV), lambda i: (0, 0)),
            pl.BlockSpec((SHORT_K, 3 * D_HYENA), lambda i: (0, 0)),
        ],
        out_specs=[
            pl.BlockSpec((tl, D_CONV), lambda i: (i, 0)),
            pl.BlockSpec((tl, D_HYENA), lambda i: (i, 0)),
            pl.BlockSpec((tl, D_HYENA), lambda i: (i, 0)),
            pl.BlockSpec((tl, D_HYENA), lambda i: (i, 0)),
        ],
        out_shape=[
            jax.ShapeDtypeStruct((T, D_CONV), jnp.bfloat16),
            jax.ShapeDtypeStruct((T, D_HYENA), jnp.float32),
            jax.ShapeDtypeStruct((T, D_HYENA), jnp.float32),
            jax.ShapeDtypeStruct((T, D_HYENA), jnp.bfloat16),
        ],
        compiler_params=_params("parallel"),
        name="mix_front",
    )(proj, proj, proj, conv_a_w, conv_h_w)


@functools.lru_cache(maxsize=None)
def _dft_tables(L):
    m = (np.arange(L, dtype=np.int64)[:, None] * np.arange(L, dtype=np.int64)[None, :]) % (2 * L)
    ang = m.astype(np.float64) * (math.pi / L)
    c = np.cos(ang).astype(np.float32)
    s = np.sin(ang).astype(np.float32)
    alt = np.zeros((SUBLANES, L), np.float32)
    alt[0] = 1.0 - 2.0 * (np.arange(L) % 2)
    return c.astype(jnp.bfloat16), s.astype(jnp.bfloat16), alt.astype(jnp.bfloat16)


@functools.lru_cache(maxsize=None)
def _filter_consts(L):
    t = np.linspace(0.0, 1.0, L, dtype=np.float32)[:, None]
    bands = (FILTER_EMB - 1) // 2
    ang = (np.float32(2.0 * math.pi / L) * np.arange(L, dtype=np.float32)[:, None]
           * np.linspace(1e-4, bands - 1, bands, dtype=np.float32)[None, :])
    z = np.zeros((L, LANES), np.float32)
    z[:, 0:1] = t
    z[:, 1:1 + bands] = np.cos(ang)
    z[:, 1 + bands:1 + 2 * bands] = -np.sin(ang)
    max_decay = math.log(DECAY_TARGET) / DECAY_FAST
    min_decay = math.log(DECAY_TARGET) / DECAY_SLOW
    deltas = np.abs(np.linspace(min_decay, max_decay, D_HYENA, dtype=np.float32))[None, :]
    return z, deltas


def _filter_body(z_ref, dl_ref, w1_ref, b1_ref, w2_ref, b2_ref, w3_ref, b3_ref, wo_ref, fr_ref,
                 gs_hi_ref, gs_lo_ref, gd_hi_ref, gd_lo_ref):
    z = z_ref[...]
    fr = fr_ref[...]
    h = jnp.sin(fr * (jnp.dot(z, w1_ref[...], precision=_HP, preferred_element_type=jnp.float32) + b1_ref[...]))
    h = jnp.sin(fr * (jnp.dot(h, w2_ref[...], precision=_HP, preferred_element_type=jnp.float32) + b2_ref[...]))
    h = jnp.sin(fr * (jnp.dot(h, w3_ref[...], precision=_HP, preferred_element_type=jnp.float32) + b3_ref[...]))
    h = jnp.dot(h, wo_ref[...], precision=_HP, preferred_element_type=jnp.float32)
    decay = jnp.exp(-z[:, 0:1] * dl_ref[...])
    hf = h[:, :D_HYENA] * decay
    hb = h[:, D_HYENA:] * decay
    rows = lax.broadcasted_iota(jnp.int32, hb.shape, 0) + pl.program_id(0) * hb.shape[0]
    hb = jnp.where(rows == 0, 0.0, hb)
    gs = hf + hb
    gd = hb - hf
    gs_hi = gs.astype(jnp.bfloat16)
    gd_hi = gd.astype(jnp.bfloat16)
    gs_hi_ref[...] = gs_hi
    gd_hi_ref[...] = gd_hi
    gs_lo_ref[...] = (gs - gs_hi.astype(jnp.float32)).astype(jnp.bfloat16)
    gd_lo_ref[...] = (gd - gd_hi.astype(jnp.float32)).astype(jnp.bfloat16)


def _pad_to(a, shape):
    return jnp.pad(a, [(0, n - s) for s, n in zip(a.shape, shape)])


def _hyena_filter_taps(L, p, tl=512):
    z, deltas = _filter_consts(L)
    P = LANES
    w1 = _pad_to(p["filt_w1"], (P, P))
    w2 = _pad_to(p["filt_w2"], (P, P))
    w3 = _pad_to(p["filt_w3"], (P, P))
    wo = _pad_to(p["filt_w_out"], (P, 2 * D_HYENA))
    b1 = _pad_to(p["filt_b1"].reshape(1, -1), (1, P))
    b2 = _pad_to(p["filt_b2"].reshape(1, -1), (1, P))
    b3 = _pad_to(p["filt_b3"].reshape(1, -1), (1, P))
    fr = _pad_to(p["filt_freq"].reshape(1, -1), (1, P))
    const = lambda i: (0, 0)
    out = jax.ShapeDtypeStruct((L, D_HYENA), jnp.bfloat16)
    return pl.pallas_call(
        _filter_body,
        grid=(L // tl,),
        in_specs=[
            pl.BlockSpec((tl, P), lambda i: (i, 0)),
            pl.BlockSpec((1, D_HYENA), const),
            pl.BlockSpec((P, P), const), pl.BlockSpec((1, P), const),
            pl.BlockSpec((P, P), const), pl.BlockSpec((1, P), const),
            pl.BlockSpec((P, P), const), pl.BlockSpec((1, P), const),
            pl.BlockSpec((P, 2 * D_HYENA), const), pl.BlockSpec((1, P), const),
        ],
        out_specs=[pl.BlockSpec((tl, D_HYENA), lambda i: (i, 0))] * 4,
        out_shape=[out] * 4,
        compiler_params=_params("parallel"),
        name="hyena_filter",
    )(z, deltas, w1, b1, w2, b2, w3, b3, wo, fr)


def _spectrum_body(L, c_ref, s_ref, alt_ref, gsh_ref, gsl_ref, gdh_ref, gdl_ref,
                   p_ref, q_ref, kn_ref):
    c = c_ref[...]
    s = s_ref[...]
    tf = c.shape[0]
    f = lax.broadcasted_iota(jnp.int32, (tf, 1), 0) + pl.program_id(1) * tf
    scale = jnp.where(f == 0, 1.0, 2.0) * (1.0 / (2 * L))
    p_ref[...] = (_bdot(c, gsh_ref[...]) + _bdot(c, gsl_ref[...])) * scale
    q_ref[...] = (_bdot(s, gdh_ref[...]) + _bdot(s, gdl_ref[...])) * scale
    alt = alt_ref[...]
    kn_ref[...] = (_bdot(alt, gsh_ref[...]) + _bdot(alt, gsl_ref[...])) * (1.0 / (2 * L))


def _filter_spectrum(L, taps, tf=512, tc=256):
    cmat, smat, alt = _dft_tables(L)
    gcol = pl.BlockSpec((L, tc), lambda j, k: (0, j))
    return pl.pallas_call(
        functools.partial(_spectrum_body, L),
        grid=(D_HYENA // tc, L // tf),
        in_specs=[
            pl.BlockSpec((tf, L), lambda j, k: (k, 0)),
            pl.BlockSpec((tf, L), lambda j, k: (k, 0)),
            pl.BlockSpec((SUBLANES, L), lambda j, k: (0, 0)),
            gcol, gcol, gcol, gcol,
        ],
        out_specs=[
            pl.BlockSpec((tf, tc), lambda j, k: (k, j)),
            pl.BlockSpec((tf, tc), lambda j, k: (k, j)),
            pl.BlockSpec((SUBLANES, tc), lambda j, k: (0, j)),
        ],
        out_shape=[
            jax.ShapeDtypeStruct((L, D_HYENA), jnp.float32),
            jax.ShapeDtypeStruct((L, D_HYENA), jnp.float32),
            jax.ShapeDtypeStruct((SUBLANES, D_HYENA), jnp.float32),
        ],
        compiler_params=_params("parallel", "arbitrary"),
        name="filter_spectrum",
    )(cmat, smat, alt, *taps)


def _fwd_dft_body(c_ref, s_ref, alt_ref, u_ref, p_ref, q_ref, kn_ref, yr_ref, z2_ref, yn_ref):
    u = u_ref[0]
    a = _bdot(c_ref[...], u)
    b = _bdot(s_ref[...], u)
    p = p_ref[...]
    q = q_ref[...]
    yr_ref[0] = (a * p + b * q).astype(yr_ref.dtype)
    z2_ref[0] = (b * p - a * q).astype(z2_ref.dtype)
    yn_ref[0] = _bdot(alt_ref[...], u) * kn_ref[...]


def _fwd_dft(u_bf16, P, Q, KN, tf, tc):
    B, L, C = u_bf16.shape
    cmat, smat, alt = _dft_tables(L)
    return pl.pallas_call(
        _fwd_dft_body,
        grid=(B, C // tc, L // tf),
        in_specs=[
            pl.BlockSpec((tf, L), lambda b, j, k: (k, 0)),
            pl.BlockSpec((tf, L), lambda b, j, k: (k, 0)),
            pl.BlockSpec((SUBLANES, L), lambda b, j, k: (0, 0)),
            pl.BlockSpec((1, L, tc), lambda b, j, k: (b, 0, j)),
            pl.BlockSpec((tf, tc), lambda b, j, k: (k, j)),
            pl.BlockSpec((tf, tc), lambda b, j, k: (k, j)),
            pl.BlockSpec((SUBLANES, tc), lambda b, j, k: (0, j)),
        ],
        out_specs=[
            pl.BlockSpec((1, tf, tc), lambda b, j, k: (b, k, j)),
            pl.BlockSpec((1, tf, tc), lambda b, j, k: (b, k, j)),
            pl.BlockSpec((1, SUBLANES, tc), lambda b, j, k: (b, 0, j)),
        ],
        out_shape=[
            jax.ShapeDtypeStruct((B, L, C), jnp.bfloat16),
            jax.ShapeDtypeStruct((B, L, C), jnp.bfloat16),
            jax.ShapeDtypeStruct((B, SUBLANES, C), jnp.float32),
        ],
        compiler_params=_params("parallel", "parallel", "arbitrary"),
        name="fwd_dft",
    )(cmat, smat, alt, u_bf16, P, Q, KN)


def _inv_dft_body(c_ref, s_ref, yr_ref, z2_ref, yn_ref, x0_ref, u_ref, skip_ref, o_ref):
    y = _bdot(c_ref[...], yr_ref[0]) + _bdot(s_ref[...], z2_ref[0])
    tt = y.shape[0]
    t = lax.broadcasted_iota(jnp.int32, (tt, 1), 0) + pl.program_id(2) * tt
    sign = (1 - 2 * (t & 1)).astype(jnp.float32)
    y = y + sign * yn_ref[0, 0:1, :]
    o_ref[0] = (x0_ref[0] * (y + u_ref[0] * skip_ref[...])).astype(o_ref.dtype)


def _inv_dft(yr, z2, yn, x0, u, skip, tt, tc):
    B, L, C = yr.shape
    cmat, smat, _ = _dft_tables(L)
    return pl.pallas_call(
        _inv_dft_body,
        grid=(B, C // tc, L // tt),
        in_specs=[
            pl.BlockSpec((tt, L), lambda b, j, k: (k, 0)),
            pl.BlockSpec((tt, L), lambda b, j, k: (k, 0)),
            pl.BlockSpec((1, L, tc), lambda b, j, k: (b, 0, j)),
            pl.BlockSpec((1, L, tc), lambda b, j, k: (b, 0, j)),
            pl.BlockSpec((1, SUBLANES, tc), lambda b, j, k: (b, 0, j)),
            pl.BlockSpec((1, tt, tc), lambda b, j, k: (b, k, j)),
            pl.BlockSpec((1, tt, tc), lambda b, j, k: (b, k, j)),
            pl.BlockSpec((1, tc), lambda b, j, k: (0, j)),
        ],
        out_specs=pl.BlockSpec((1, tt, tc), lambda b, j, k: (b, k, j)),
        out_shape=jax.ShapeDtypeStruct((B, L, C), jnp.bfloat16),
        compiler_params=_params("parallel", "parallel", "arbitrary"),
        name="inv_dft",
    )(cmat, smat, yr, z2, yn, x0, u, skip.reshape(1, C))


ROW_SPLIT_WIDTH = LANES


def _load_row_split(ref, n, rpt):
    return jnp.concatenate([ref[pl.ds(k, n, stride=rpt), :] for k in range(rpt)], axis=1)


def _branch_merge_body(ya_ref, yb_ref, ga_ref, gb_ref, wa_ref, wh_ref, m_ref):
    pa = _bdot(ya_ref[...], wa_ref[...])
    ph = _bdot(yb_ref[...], wh_ref[...])
    merged = jax.nn.sigmoid(ga_ref[...]) * pa + jax.nn.sigmoid(gb_ref[...]) * ph
    m_ref[...] = merged.astype(m_ref.dtype)


def _branch_merge(ya, yb, proj, wa, wh, tm=512):
    T = ya.shape[0]
    D = wa.shape[1]
    const = lambda i: (0, 0)
    row = lambda i: (i, 0)
    one = pl.Buffered(1)
    ga_blk = G_OFF // D
    return pl.pallas_call(
        _branch_merge_body,
        grid=(T // tm,),
        in_specs=[
            pl.BlockSpec((tm, D_CONV), row),
            pl.BlockSpec((tm, D_HYENA), row),
            pl.BlockSpec((tm, D), lambda i: (i, ga_blk)),
            pl.BlockSpec((tm, D), lambda i: (i, ga_blk + 1)),
            pl.BlockSpec((D_CONV, D), const, pipeline_mode=one),
            pl.BlockSpec((D_HYENA, D), const, pipeline_mode=one),
        ],
        out_specs=pl.BlockSpec((tm, D), row),
        out_shape=jax.ShapeDtypeStruct((T, D), jnp.bfloat16),
        compiler_params=_params("parallel"),
        name="branch_merge",
    )(ya, yb, proj, proj, wa, wh)


def _merge_body(m_ref, x_ref, wo_ref, gn_ref, wrt_ref, h_ref, xn_ref, aff_ref):
    h = x_ref[...] + _bdot(m_ref[...], wo_ref[...])
    h_ref[...] = h
    xn = _rms(h, gn_ref[...])
    xn_ref[...] = xn
    lg = lax.dot_general(wrt_ref[...], xn, (((1,), (1,)), ((), ())),
                         preferred_element_type=jnp.float32, precision=_HP)
    ex = jnp.exp(lg - jnp.max(lg, axis=0, keepdims=True))
    aff_ref[...] = ex / jnp.sum(ex, axis=0, keepdims=True)


def _merge_out(merged, x, wo, g_ffn, w_router_t, tm=512):
    T, D = x.shape
    const = lambda i: (0, 0)
    row = lambda i: (i, 0)
    one = pl.Buffered(1)
    return pl.pallas_call(
        _merge_body,
        grid=(T // tm,),
        in_specs=[
            pl.BlockSpec((tm, D), row),
            pl.BlockSpec((tm, D), row),
            pl.BlockSpec((D, D), const, pipeline_mode=one),
            pl.BlockSpec((1, D), const, pipeline_mode=one),
            pl.BlockSpec((N_EXPERTS, D), const, pipeline_mode=one),
        ],
        out_specs=[
            pl.BlockSpec((tm, D), row),
            pl.BlockSpec((tm, D), row),
            pl.BlockSpec((N_EXPERTS, tm), lambda i: (0, i)),
        ],
        out_shape=[
            jax.ShapeDtypeStruct((T, D), jnp.float32),
            jax.ShapeDtypeStruct((T, D), jnp.float32),
            jax.ShapeDtypeStruct((N_EXPERTS, T), jnp.float32),
        ],
        compiler_params=_params("parallel"),
        name="merge_out",
    )(merged, x, wo, g_ffn.reshape(1, D), w_router_t)


def _route_body(cap, aff_ref, idx_ref, g_ref, starts_ref):
    f32, bf16 = jnp.float32, jnp.bfloat16
    x = aff_ref[0]
    R = x.shape[0]
    bits = pltpu.bitcast(x, jnp.int32)

    def count(m):
        return jnp.sum(jnp.where(m, 1.0, 0.0), keepdims=True)

    def bit_step(i, cur):
        cand = cur | jnp.left_shift(jnp.int32(1), 30 - i)
        return jnp.where(count(bits >= cand) >= cap, cand, cur)

    thr = lax.fori_loop(0, 31, bit_step, jnp.zeros((1, 1), jnp.int32))

    def tri(n, cmp):
        a = lax.broadcasted_iota(jnp.int32, (n, n), 0)
        b = lax.broadcasted_iota(jnp.int32, (n, n), 1)
        return jnp.where(cmp(a, b), 1.0, 0.0).astype(bf16)

    upper = tri(LANES, lambda k, l: k <= l)
    ones = jnp.ones((LANES, LANES), bf16)
    strict_lower = tri(R, lambda r, k: k < r)

    def prefix(m01):
        mb = m01.astype(bf16)
        rowcum = _bdot(mb, upper)
        rowtot = _bdot(mb, ones)
        before = _bdot(strict_lower, rowtot.astype(bf16))
        return mb, rowcum, rowtot, before

    gt = jnp.where(bits > thr, 1.0, 0.0)
    eq = jnp.where(bits == thr, 1.0, 0.0)
    _, eq_cum, _, eq_before = prefix(eq)
    need = cap - jnp.sum(gt, keepdims=True)
    sel = gt + eq * jnp.where(eq_cum + eq_before <= need, 1.0, 0.0)
    selb, rowcum, rowtot, before = prefix(sel)

    through_row = (before + rowtot)[:, 0:1]
    j = lax.broadcasted_iota(jnp.int32, (1, cap), 1).astype(f32)
    done = through_row <= j
    r_j = jnp.sum(jnp.where(done, 1.0, 0.0), axis=0, keepdims=True)
    before_j = jnp.sum(jnp.where(done, rowtot[:, 0:1], 0.0), axis=0, keepdims=True)
    rows = lax.broadcasted_iota(jnp.int32, (R, cap), 0).astype(f32)
    pick_row = jnp.where(rows == r_j, 1.0, 0.0).astype(bf16)

    nt = (((1,), (1,)), ((), ()))
    lower = tri(LANES, lambda l, k: k <= l)
    rowcum_t = lax.dot_general(lower, selb, nt, preferred_element_type=f32)
    cum_j = _bdot(rowcum_t.astype(bf16), pick_row)
    lane_j = jnp.sum(jnp.where(cum_j <= j - before_j, 1.0, 0.0), axis=0, keepdims=True)
    idx_ref[0] = (r_j * LANES + lane_j).astype(jnp.int32)

    rowtot_t = lax.dot_general(jnp.ones((SUBLANES, LANES), bf16), selb, nt, preferred_element_type=f32)
    strict_upper = tri(R, lambda k, r: k < r)
    starts_ref[0] = _bdot(rowtot_t.astype(bf16), strict_upper).astype(jnp.int32)

    eye = tri(LANES, lambda a, b: a == b)
    hi = x.astype(bf16)
    r1 = x - hi.astype(f32)
    mid = r1.astype(bf16)
    lo = (r1 - mid.astype(f32)).astype(bf16)

    def pick(part):
        part_t = lax.dot_general(eye, part, nt, preferred_element_type=f32)
        return _bdot(part_t.astype(bf16), pick_row)

    aff_j = (pick(hi) + pick(mid)) + pick(lo)
    lanes = lax.broadcasted_iota(jnp.int32, (LANES, cap), 0).astype(f32)
    g_ref[0] = jnp.sum(jnp.where(lanes == lane_j, aff_j, 0.0), axis=0, keepdims=True)


def _route(aff_t, cap):
    E, T = aff_t.shape
    R = T // LANES
    return pl.pallas_call(
        functools.partial(_route_body, cap),
        grid=(E,),
        in_specs=[pl.BlockSpec((1, R, LANES), lambda e: (e, 0, 0))],
        out_specs=[pl.BlockSpec((1, 1, cap), lambda e: (e, 0, 0)),
                   pl.BlockSpec((1, 1, cap), lambda e: (e, 0, 0)),
                   pl.BlockSpec((1, SUBLANES, R), lambda e: (e, 0, 0))],
        out_shape=[jax.ShapeDtypeStruct((E, 1, cap), jnp.int32),
                   jax.ShapeDtypeStruct((E, 1, cap), jnp.float32),
                   jax.ShapeDtypeStruct((E, SUBLANES, R), jnp.int32)],
        compiler_params=_params("parallel"),
        name="route",
    )(aff_t.reshape(E, R, LANES))


_SEM_X, _SEM_OUT, _SEM_IDX = range(3)


def _moe_body(tm, rpt, n_tiles, n_f, idx_hbm, g_ref, wg_ref, wu_ref, wd_ref, xn_hbm, y_hbm,
              idx_smem, xrows, yrows, xb, acc, sem):
    n = pl.program_id(0) * pl.num_programs(1) + pl.program_id(1)
    f = pl.program_id(2)
    last = n_f - 1
    share = xrows.shape[0] // n_f
    next_tile = jnp.minimum(n + 1, n_tiles - 1)

    def load_idx(tile):
        cp = pltpu.make_async_copy(idx_hbm.at[tile], idx_smem, sem.at[_SEM_IDX])
        cp.start()
        cp.wait()

    def start_row(j):
        t = idx_smem[jnp.minimum(j, tm - 1)]
        pltpu.make_async_copy(xn_hbm.at[pl.ds(t, 1), :], xrows.at[pl.ds(j, 1), :], sem.at[_SEM_X]).start()

    x_landed = pltpu.make_async_copy(xn_hbm.at[pl.ds(0, xrows.shape[0]), :], xrows, sem.at[_SEM_X])
    tile_rows = pl.ds(pl.multiple_of(n * (tm * rpt), tm * rpt), tm * rpt)
    y_out = pltpu.make_async_copy(yrows, y_hbm.at[tile_rows, :], sem.at[_SEM_OUT])

    @pl.when(f == 0)
    def _():
        @pl.when(n == 0)
        def _():
            load_idx(0)

            def body(j, c):
                start_row(j)
                return c
            lax.fori_loop(0, xrows.shape[0], body, 0, unroll=8)

        x_landed.wait()
        xb[...] = xrows[pl.ds(0, tm), :].astype(jnp.bfloat16)
        acc[...] = jnp.zeros_like(acc)
        load_idx(next_tile)

    for u in range(share):
        start_row(f * share + u)

    x = xb[...]
    a = _bdot(x, wg_ref[0].astype(jnp.bfloat16))
    b = _bdot(x, wu_ref[0].astype(jnp.bfloat16))
    hmid = (a * jax.nn.sigmoid(a) * b).astype(jnp.bfloat16)
    acc[...] += _bdot(hmid, wd_ref[0].astype(jnp.bfloat16))

    @pl.when(f == last)
    def _():
        @pl.when(n > 0)
        def _():
            y_out.wait()

        g = g_ref[0]
        for k in range(rpt):
            yrows[pl.ds(k, tm, stride=rpt), :] = acc[:, k * LANES:(k + 1) * LANES] * g
        y_out.start()

        @pl.when(n == n_tiles - 1)
        def _():
            y_out.wait()
            x_landed.wait()


def _moe(idx, g, xn, w_gate, w_up, w_down, tf=256):
    E, _, C = idx.shape
    D, F = w_gate.shape[1], w_gate.shape[2]
    rpt = D // ROW_SPLIT_WIDTH
    tm = min(C, 1024)
    any_spec = pl.BlockSpec(memory_space=pl.ANY)
    n_tiles = E * (C // tm)
    n_f = F // tf
    gather_rows = -(-tm // (n_f * SUBLANES)) * SUBLANES * n_f
    return pl.pallas_call(
        functools.partial(_moe_body, tm, rpt, n_tiles, n_f),
        grid=(E, C // tm, n_f),
        in_specs=[
            any_spec,
            pl.BlockSpec((1, tm, 1), lambda e, i, f: (e, i, 0)),
            pl.BlockSpec((1, D, tf), lambda e, i, f: (e, 0, f)),
            pl.BlockSpec((1, D, tf), lambda e, i, f: (e, 0, f)),
            pl.BlockSpec((1, tf, D), lambda e, i, f: (e, f, 0)),
            any_spec,
        ],
        out_specs=any_spec,
        out_shape=jax.ShapeDtypeStruct((E * C * rpt, ROW_SPLIT_WIDTH), jnp.float32),
        scratch_shapes=[
            pltpu.SMEM((tm,), jnp.int32),
            pltpu.VMEM((gather_rows, D), jnp.float32),
            pltpu.VMEM((tm * rpt, ROW_SPLIT_WIDTH), jnp.float32),
            pltpu.VMEM((tm, D), jnp.bfloat16),
            pltpu.VMEM((tm, D), jnp.float32),
            pltpu.SemaphoreType.DMA((3,)),
        ],
        compiler_params=pltpu.CompilerParams(
            dimension_semantics=("arbitrary", "arbitrary", "arbitrary"),
            vmem_limit_bytes=VMEM_LIMIT_BYTES, has_side_effects=True),
        name="moe",
    )(idx.reshape(n_tiles, tm), g.reshape(E, C, 1), w_gate, w_up, w_down, xn)


_COMBINE_UNROLL = 4


def _combine_body(tt, rpt, cap, n_exp, n_rows, ch, idx_ref, starts_ref, h_ref, y_hbm, gn_ref, o_ref,
                  ybuf, acc, sem):
    i = pl.program_id(0)
    slot = i % 2
    t0 = i * tt
    total = n_exp * cap
    rows_per_tile = tt // LANES

    def lower_bound(e, row):
        s = starts_ref[e * n_rows + jnp.minimum(row, n_rows - 1)]
        return jnp.where(row < n_rows, s, cap)

    def run_of(e, tile):
        a = lower_bound(e, tile * rows_per_tile)
        b = lower_bound(e, (tile + 1) * rows_per_tile)
        return e * cap + a, b - a

    def chunk_copy(e, first_slot, s):
        start = jnp.minimum(first_slot, total - ch)
        src = y_hbm.at[pl.ds(pl.multiple_of(start * rpt, rpt), ch * rpt), :]
        return pltpu.make_async_copy(src, ybuf.at[s, e], sem.at[s, e]), first_slot - start

    def fetch_first_chunks(tile, s):
        for e in range(n_exp):
            chunk_copy(e, run_of(e, tile)[0], s)[0].start()

    @pl.when(i == 0)
    def _():
        fetch_first_chunks(0, 0)

    @pl.when(i + 1 < pl.num_programs(0))
    def _():
        fetch_first_chunks(i + 1, 1 - slot)

    acc[...] = jnp.zeros_like(acc)

    def add_rows(e, first_slot, off, m):
        def token_rows(r):
            t = idx_ref[first_slot + r] - t0
            return pl.ds(pl.multiple_of(t * rpt, rpt), rpt)

        def y_rows(r):
            return pl.ds(pl.multiple_of((off + r) * rpt, rpt), rpt)

        def group(q, c):
            r0 = q * _COMBINE_UNROLL
            dst = [token_rows(r0 + u) for u in range(_COMBINE_UNROLL)]
            sums = [acc[dst[u], :] + ybuf[slot, e, y_rows(r0 + u), :] for u in range(_COMBINE_UNROLL)]
            for u in range(_COMBINE_UNROLL):
                acc[dst[u], :] = sums[u]
            return c

        full = m // _COMBINE_UNROLL
        lax.fori_loop(0, full, group, 0)

        def single(r, c):
            d = token_rows(r)
            acc[d, :] = acc[d, :] + ybuf[slot, e, y_rows(r), :]
            return c

        lax.fori_loop(full * _COMBINE_UNROLL, m, single, 0)

    for e in range(n_exp):
        first_slot, count = run_of(e, i)
        cp, off = chunk_copy(e, first_slot, slot)
        cp.wait()
        m0 = jnp.minimum(count, ch - off)
        add_rows(e, first_slot, off, m0)

        def more(done):
            cp2, off2 = chunk_copy(e, first_slot + done, slot)
            cp2.start()
            cp2.wait()
            m = jnp.minimum(count - done, ch - off2)
            add_rows(e, first_slot + done, off2, m)
            return done + m

        lax.while_loop(lambda done: done < count, more, m0)

    o_ref[...] = _rms(h_ref[...] + _load_row_split(acc, tt, rpt), gn_ref[...])


def _combine_norm(idx, starts, h, y_rs, g_final, tt=256, ch=64):
    E, _, C = idx.shape
    T, D = h.shape
    rpt = D // ROW_SPLIT_WIDTH
    n_rows = T // LANES
    grid_spec = pltpu.PrefetchScalarGridSpec(
        num_scalar_prefetch=2,
        grid=(T // tt,),
        in_specs=[
            pl.BlockSpec((tt, D), lambda i, *_: (i, 0)),
            pl.BlockSpec(memory_space=pl.ANY),
            pl.BlockSpec((1, D), lambda i, *_: (0, 0)),
        ],
        out_specs=pl.BlockSpec((tt, D), lambda i, *_: (i, 0)),
        scratch_shapes=[
            pltpu.VMEM((2, E, ch * rpt, ROW_SPLIT_WIDTH), jnp.float32),
            pltpu.VMEM((tt * rpt, ROW_SPLIT_WIDTH), jnp.float32),
            pltpu.SemaphoreType.DMA((2, E)),
        ],
    )
    return pl.pallas_call(
        functools.partial(_combine_body, tt, rpt, C, E, n_rows, ch),
        grid_spec=grid_spec,
        out_shape=jax.ShapeDtypeStruct((T, D), jnp.float32),
        compiler_params=_params("arbitrary"),
        name="combine_norm",
    )(idx.reshape(E * C), starts[:, 0, :].reshape(E * n_rows), h, y_rs, g_final.reshape(1, D))


def _trunk(x, p):
    B, L, D = x.shape
    T = B * L
    xf = x.reshape(T, D)
    proj = _norm_proj(xf, p["norm_mix"], p["w_in"])
    ya, x0, u, ub = _mix_front(proj, p["conv_a_w"], p["conv_h_w"], L)

    taps = _hyena_filter_taps(L, p)
    P, Q, KN = _filter_spectrum(L, taps)
    tc = 1024 if L <= 2048 else 512
    shp = (B, L, D_HYENA)
    yr, z2, yn = _fwd_dft(ub.reshape(shp), P, Q, KN, tf=512, tc=tc)
    yb = _inv_dft(yr, z2, yn, x0.reshape(shp), u.reshape(shp), p["hyena_skip"], tt=512, tc=tc)

    merged = _branch_merge(ya, yb.reshape(T, D_HYENA), proj, p["w_proj_a"], p["w_proj_h"])
    h, xn, aff_t = _merge_out(merged, xf, p["w_out"], p["norm_ffn"], p["w_router"].T)

    cap = max(1, CAPACITY_FACTOR * T // N_EXPERTS)
    idx, g, starts = _route(aff_t, cap)
    y_rs = _moe(idx, g, xn, p["w_gate"], p["w_up"], p["w_down"])
    return _combine_norm(idx, starts, h, y_rs, p["norm_final"]).reshape(B, L, D)


def kernel(x_prompt, x_sample, w_in, conv_a_w, conv_h_w, filt_w1, filt_b1, filt_w2, filt_b2, filt_w3, filt_b3, filt_w_out, filt_freq, hyena_skip, w_proj_a, w_proj_h, w_out, norm_mix, norm_ffn, w_router, w_gate, w_up, w_down, norm_final):
    bf16 = jnp.bfloat16
    p = dict(
        w_in=w_in[0].astype(bf16), conv_a_w=conv_a_w[0], conv_h_w=conv_h_w[0],
        filt_w1=filt_w1[0], filt_b1=filt_b1[0], filt_w2=filt_w2[0], filt_b2=filt_b2[0],
        filt_w3=filt_w3[0], filt_b3=filt_b3[0], filt_w_out=filt_w_out[0], filt_freq=filt_freq[0],
        hyena_skip=hyena_skip[0], w_proj_a=w_proj_a[0].astype(bf16), w_proj_h=w_proj_h[0].astype(bf16),
        w_out=w_out[0].astype(bf16), norm_mix=norm_mix[0], norm_ffn=norm_ffn[0],
        w_router=w_router[0], w_gate=w_gate[0], w_up=w_up[0], w_down=w_down[0],
        norm_final=norm_final,
    )
    return (_trunk(x_prompt, p), _trunk(x_sample, p))
```

```python
import functools
import math

import numpy as np

import jax
import jax.numpy as jnp
from jax import lax
from jax.experimental import pallas as pl
from jax.experimental.pallas import tpu as pltpu

D_MODEL = 2048
D_CONV = 1024
D_HYENA = 1024
SHORT_K = 3
FILTER_EMB = 33
FILTER_ORDER = 64
DECAY_FAST = 0.3
DECAY_SLOW = 1.5
DECAY_TARGET = 1e-2
N_EXPERTS = 16
CAPACITY_FACTOR = 2
D_EXPERT = 5632
EPS = 1e-6
D_IN = 3 * D_CONV + 3 * D_HYENA + 2 * D_MODEL
H_OFF = 3 * D_CONV
G_OFF = 3 * D_CONV + 3 * D_HYENA

LANES = 128
SUBLANES = 8
VMEM_LIMIT_BYTES = 56 * 1024 * 1024

_HP = lax.Precision.HIGHEST


def _rms(x, g):
    r = lax.rsqrt(jnp.mean(x * x, axis=-1, keepdims=True) + EPS)
    return (x * r) * g


def _bdot(a, b):
    return jnp.dot(a, b, preferred_element_type=jnp.float32)


def _params(*sem):
    return pltpu.CompilerParams(dimension_semantics=sem, vmem_limit_bytes=VMEM_LIMIT_BYTES)


def _norm_proj_body(x_ref, g_ref, w_ref, o_ref, xn_ref):
    @pl.when(pl.program_id(1) == 0)
    def _():
        xn_ref[...] = _rms(x_ref[...], g_ref[...]).astype(jnp.bfloat16)

    o_ref[...] = _bdot(xn_ref[...], w_ref[...])


def _norm_proj(x, g, w_bf16, tm=1024, tn=1024):
    T, D = x.shape
    N = w_bf16.shape[1]
    return pl.pallas_call(
        _norm_proj_body,
        grid=(T // tm, N // tn),
        in_specs=[
            pl.BlockSpec((tm, D), lambda i, j: (i, 0)),
            pl.BlockSpec((1, D), lambda i, j: (0, 0)),
            pl.BlockSpec((D, tn), lambda i, j: (0, j)),
        ],
        out_specs=pl.BlockSpec((tm, tn), lambda i, j: (i, j)),
        out_shape=jax.ShapeDtypeStruct((T, N), jnp.float32),
        scratch_shapes=[pltpu.VMEM((tm, D), jnp.bfloat16)],
        compiler_params=_params("parallel", "arbitrary"),
        name="norm_proj",
    )(x, g.reshape(1, D), w_bf16)


def _shift_down(x, first_row):
    rows = lax.broadcasted_iota(jnp.int32, x.shape, 0)
    return jnp.where(rows == 0, first_row, pltpu.roll(x, 1, axis=0))


def _shift_up(x, last_row):
    n = x.shape[0]
    rows = lax.broadcasted_iota(jnp.int32, x.shape, 0)
    return jnp.where(rows == n - 1, last_row, pltpu.roll(x, n - 1, axis=0))


def _mix_body(tiles_per_seq, p_ref, pp_ref, pn_ref, wa_ref, wh_ref,
              ya_ref, x0_ref, u_ref, ub_ref):
    i = pl.program_id(0)
    keep_prev = jnp.where(i % tiles_per_seq == 0, 0.0, 1.0)
    keep_next = jnp.where(i % tiles_per_seq == tiles_per_seq - 1, 0.0, 1.0)
    C = D_CONV
    q = p_ref[:, 0:C] * p_ref[:, 2 * C:3 * C]
    q_prev = pp_ref[SUBLANES - 1:SUBLANES, 0:C] * pp_ref[SUBLANES - 1:SUBLANES, 2 * C:3 * C] * keep_prev
    q_next = pn_ref[0:1, 0:C] * pn_ref[0:1, 2 * C:3 * C] * keep_next
    conv = (_shift_down(q, q_prev) * wa_ref[0:1, :] + q * wa_ref[1:2, :]
            + _shift_up(q, q_next) * wa_ref[2:3, :])
    ya_ref[...] = (p_ref[:, C:2 * C] * conv).astype(ya_ref.dtype)

    def hconv(j):
        lo, hi = H_OFF + j * D_HYENA, H_OFF + (j + 1) * D_HYENA
        wlo, whi = j * D_HYENA, (j + 1) * D_HYENA
        x = p_ref[:, lo:hi]
        xp = pp_ref[SUBLANES - 1:SUBLANES, lo:hi] * keep_prev
        xn = pn_ref[0:1, lo:hi] * keep_next
        return (_shift_down(x, xp) * wh_ref[0:1, wlo:whi] + x * wh_ref[1:2, wlo:whi]
                + _shift_up(x, xn) * wh_ref[2:3, wlo:whi])

    x0_ref[...] = hconv(0)
    u = hconv(2) * hconv(1)
    u_ref[...] = u
    ub_ref[...] = u.astype(ub_ref.dtype)


def _mix_front(proj, conv_a_w, conv_h_w, L, tl=256):
    T = proj.shape[0]
    W = G_OFF
    nb = tl // SUBLANES
    last_blk = T // SUBLANES - 1
    return pl.pallas_call(
        functools.partial(_mix_body, L // tl),
        grid=(T // tl,),
        in_specs=[
            pl.BlockSpec((tl, W), lambda i: (i, 0)),
            pl.BlockSpec((SUBLANES, W), lambda i: (jnp.maximum(i * nb - 1, 0), 0)),
            pl.BlockSpec((SUBLANES, W), lambda i: (jnp.minimum((i + 1) * nb, last_blk), 0)),
            pl.BlockSpec((SHORT_K, D_CONV), lambda i: (0, 0)),
            pl.BlockSpec((SHORT_K, 3 * D_HYENA), lambda i: (0, 0)),
        ],
        out_specs=[
            pl.BlockSpec((tl, D_CONV), lambda i: (i, 0)),
            pl.BlockSpec((tl, D_HYENA), lambda i: (i, 0)),
            pl.BlockSpec((tl, D_HYENA), lambda i: (i, 0)),
            pl.BlockSpec((tl, D_HYENA), lambda i: (i, 0)),
        ],
        out_shape=[
            jax.ShapeDtypeStruct((T, D_CONV), jnp.bfloat16),
            jax.ShapeDtypeStruct((T, D_HYENA), jnp.float32),
            jax.ShapeDtypeStruct((T, D_HYENA), jnp.float32),
            jax.ShapeDtypeStruct((T, D_HYENA), jnp.bfloat16),
        ],
        compiler_params=_params("parallel"),
        name="mix_front",
    )(proj, proj, proj, conv_a_w, conv_h_w)


@functools.lru_cache(maxsize=None)
def _dft_tables(L):
    m = (np.arange(L, dtype=np.int64)[:, None] * np.arange(L, dtype=np.int64)[None, :]) % (2 * L)
    ang = m.astype(np.float64) * (math.pi / L)
    c = np.cos(ang).astype(np.float32)
    s = np.sin(ang).astype(np.float32)
    alt = np.zeros((SUBLANES, L), np.float32)
    alt[0] = 1.0 - 2.0 * (np.arange(L) % 2)
    return c.astype(jnp.bfloat16), s.astype(jnp.bfloat16), alt.astype(jnp.bfloat16)


@functools.lru_cache(maxsize=None)
def _filter_consts(L):
    t = np.linspace(0.0, 1.0, L, dtype=np.float32)[:, None]
    bands = (FILTER_EMB - 1) // 2
    ang = (np.float32(2.0 * math.pi / L) * np.arange(L, dtype=np.float32)[:, None]
           * np.linspace(1e-4, bands - 1, bands, dtype=np.float32)[None, :])
    z = np.zeros((L, LANES), np.float32)
    z[:, 0:1] = t
    z[:, 1:1 + bands] = np.cos(ang)
    z[:, 1 + bands:1 + 2 * bands] = -np.sin(ang)
    max_decay = math.log(DECAY_TARGET) / DECAY_FAST
    min_decay = math.log(DECAY_TARGET) / DECAY_SLOW
    deltas = np.abs(np.linspace(min_decay, max_decay, D_HYENA, dtype=np.float32))[None, :]
    return z, deltas


def _filter_body(z_ref, dl_ref, w1_ref, b1_ref, w2_ref, b2_ref, w3_ref, b3_ref, wo_ref, fr_ref,
                 gs_hi_ref, gs_lo_ref, gd_hi_ref, gd_lo_ref):
    z = z_ref[...]
    fr = fr_ref[...]
    h = jnp.sin(fr * (jnp.dot(z, w1_ref[...], precision=_HP, preferred_element_type=jnp.float32) + b1_ref[...]))
    h = jnp.sin(fr * (jnp.dot(h, w2_ref[...], precision=_HP, preferred_element_type=jnp.float32) + b2_ref[...]))
    h = jnp.sin(fr * (jnp.dot(h, w3_ref[...], precision=_HP, preferred_element_type=jnp.float32) + b3_ref[...]))
    h = jnp.dot(h, wo_ref[...], precision=_HP, preferred_element_type=jnp.float32)
    decay = jnp.exp(-z[:, 0:1] * dl_ref[...])
    hf = h[:, :D_HYENA] * decay
    hb = h[:, D_HYENA:] * decay
    rows = lax.broadcasted_iota(jnp.int32, hb.shape, 0) + pl.program_id(0) * hb.shape[0]
    hb = jnp.where(rows == 0, 0.0, hb)
    gs = hf + hb
    gd = hb - hf
    gs_hi = gs.astype(jnp.bfloat16)
    gd_hi = gd.astype(jnp.bfloat16)
    gs_hi_ref[...] = gs_hi
    gd_hi_ref[...] = gd_hi
    gs_lo_ref[...] = (gs - gs_hi.astype(jnp.float32)).astype(jnp.bfloat16)
    gd_lo_ref[...] = (gd - gd_hi.astype(jnp.float32)).astype(jnp.bfloat16)


def _pad_to(a, shape):
    return jnp.pad(a, [(0, n - s) for s, n in zip(a.shape, shape)])


def _hyena_filter_taps(L, p, tl=512):
    z, deltas = _filter_consts(L)
    P = LANES
    w1 = _pad_to(p["filt_w1"], (P, P))
    w2 = _pad_to(p["filt_w2"], (P, P))
    w3 = _pad_to(p["filt_w3"], (P, P))
    wo = _pad_to(p["filt_w_out"], (P, 2 * D_HYENA))
    b1 = _pad_to(p["filt_b1"].reshape(1, -1), (1, P))
    b2 = _pad_to(p["filt_b2"].reshape(1, -1), (1, P))
    b3 = _pad_to(p["filt_b3"].reshape(1, -1), (1, P))
    fr = _pad_to(p["filt_freq"].reshape(1, -1), (1, P))
    const = lambda i: (0, 0)
    out = jax.ShapeDtypeStruct((L, D_HYENA), jnp.bfloat16)
    return pl.pallas_call(
        _filter_body,
        grid=(L // tl,),
        in_specs=[
            pl.BlockSpec((tl, P), lambda i: (i, 0)),
            pl.BlockSpec((1, D_HYENA), const),
            pl.BlockSpec((P, P), const), pl.BlockSpec((1, P), const),
            pl.BlockSpec((P, P), const), pl.BlockSpec((1, P), const),
            pl.BlockSpec((P, P), const), pl.BlockSpec((1, P), const),
            pl.BlockSpec((P, 2 * D_HYENA), const), pl.BlockSpec((1, P), const),
        ],
        out_specs=[pl.BlockSpec((tl, D_HYENA), lambda i: (i, 0))] * 4,
        out_shape=[out] * 4,
        compiler_params=_params("parallel"),
        name="hyena_filter",
    )(z, deltas, w1, b1, w2, b2, w3, b3, wo, fr)


def _spectrum_body(L, c_ref, s_ref, alt_ref, gsh_ref, gsl_ref, gdh_ref, gdl_ref,
                   p_ref, q_ref, kn_ref):
    c = c_ref[...]
    s = s_ref[...]
    tf = c.shape[0]
    f = lax.broadcasted_iota(jnp.int32, (tf, 1), 0) + pl.program_id(1) * tf
    scale = jnp.where(f == 0, 1.0, 2.0) * (1.0 / (2 * L))
    p_ref[...] = (_bdot(c, gsh_ref[...]) + _bdot(c, gsl_ref[...])) * scale
    q_ref[...] = (_bdot(s, gdh_ref[...]) + _bdot(s, gdl_ref[...])) * scale
    alt = alt_ref[...]
    kn_ref[...] = (_bdot(alt, gsh_ref[...]) + _bdot(alt, gsl_ref[...])) * (1.0 / (2 * L))


def _filter_spectrum(L, taps, tf=512, tc=256):
    cmat, smat, alt = _dft_tables(L)
    gcol = pl.BlockSpec((L, tc), lambda j, k: (0, j))
    return pl.pallas_call(
        functools.partial(_spectrum_body, L),
        grid=(D_HYENA // tc, L // tf),
        in_specs=[
            pl.BlockSpec((tf, L), lambda j, k: (k, 0)),
            pl.BlockSpec((tf, L), lambda j, k: (k, 0)),
            pl.BlockSpec((SUBLANES, L), lambda j, k: (0, 0)),
            gcol, gcol, gcol, gcol,
        ],
        out_specs=[
            pl.BlockSpec((tf, tc), lambda j, k: (k, j)),
            pl.BlockSpec((tf, tc), lambda j, k: (k, j)),
            pl.BlockSpec((SUBLANES, tc), lambda j, k: (0, j)),
        ],
        out_shape=[
            jax.ShapeDtypeStruct((L, D_HYENA), jnp.float32),
            jax.ShapeDtypeStruct((L, D_HYENA), jnp.float32),
            jax.ShapeDtypeStruct((SUBLANES, D_HYENA), jnp.float32),
        ],
        compiler_params=_params("parallel", "arbitrary"),
        name="filter_spectrum",
    )(cmat, smat, alt, *taps)


def _fwd_dft_body(c_ref, s_ref, alt_ref, u_ref, p_ref, q_ref, kn_ref, yr_ref, z2_ref, yn_ref):
    u = u_ref[0]
    a = _bdot(c_ref[...], u)
    b = _bdot(s_ref[...], u)
    p = p_ref[...]
    q = q_ref[...]
    yr_ref[0] = (a * p + b * q).astype(yr_ref.dtype)
    z2_ref[0] = (b * p - a * q).astype(z2_ref.dtype)
    yn_ref[0] = _bdot(alt_ref[...], u) * kn_ref[...]


def _fwd_dft(u_bf16, P, Q, KN, tf, tc):
    B, L, C = u_bf16.shape
    cmat, smat, alt = _dft_tables(L)
    return pl.pallas_call(
        _fwd_dft_body,
        grid=(B, C // tc, L // tf),
        in_specs=[
            pl.BlockSpec((tf, L), lambda b, j, k: (k, 0)),
            pl.BlockSpec((tf, L), lambda b, j, k: (k, 0)),
            pl.BlockSpec((SUBLANES, L), lambda b, j, k: (0, 0)),
            pl.BlockSpec((1, L, tc), lambda b, j, k: (b, 0, j)),
            pl.BlockSpec((tf, tc), lambda b, j, k: (k, j)),
            pl.BlockSpec((tf, tc), lambda b, j, k: (k, j)),
            pl.BlockSpec((SUBLANES, tc), lambda b, j, k: (0, j)),
        ],
        out_specs=[
            pl.BlockSpec((1, tf, tc), lambda b, j, k: (b, k, j)),
            pl.BlockSpec((1, tf, tc), lambda b, j, k: (b, k, j)),
            pl.BlockSpec((1, SUBLANES, tc), lambda b, j, k: (b, 0, j)),
        ],
        out_shape=[
            jax.ShapeDtypeStruct((B, L, C), jnp.bfloat16),
            jax.ShapeDtypeStruct((B, L, C), jnp.bfloat16),
            jax.ShapeDtypeStruct((B, SUBLANES, C), jnp.float32),
        ],
        compiler_params=_params("parallel", "parallel", "arbitrary"),
        name="fwd_dft",
    )(cmat, smat, alt, u_bf16, P, Q, KN)


def _inv_dft_body(c_ref, s_ref, yr_ref, z2_ref, yn_ref, x0_ref, u_ref, skip_ref, o_ref):
    y = _bdot(c_ref[...], yr_ref[0]) + _bdot(s_ref[...], z2_ref[0])
    tt = y.shape[0]
    t = lax.broadcasted_iota(jnp.int32, (tt, 1), 0) + pl.program_id(2) * tt
    sign = (1 - 2 * (t & 1)).astype(jnp.float32)
    y = y + sign * yn_ref[0, 0:1, :]
    o_ref[0] = (x0_ref[0] * (y + u_ref[0] * skip_ref[...])).astype(o_ref.dtype)


def _inv_dft(yr, z2, yn, x0, u, skip, tt, tc):
    B, L, C = yr.shape
    cmat, smat, _ = _dft_tables(L)
    return pl.pallas_call(
        _inv_dft_body,
        grid=(B, C // tc, L // tt),
        in_specs=[
            pl.BlockSpec((tt, L), lambda b, j, k: (k, 0)),
            pl.BlockSpec((tt, L), lambda b, j, k: (k, 0)),
            pl.BlockSpec((1, L, tc), lambda b, j, k: (b, 0, j)),
            pl.BlockSpec((1, L, tc), lambda b, j, k: (b, 0, j)),
            pl.BlockSpec((1, SUBLANES, tc), lambda b, j, k: (b, 0, j)),
            pl.BlockSpec((1, tt, tc), lambda b, j, k: (b, k, j)),
            pl.BlockSpec((1, tt, tc), lambda b, j, k: (b, k, j)),
            pl.BlockSpec((1, tc), lambda b, j, k: (0, j)),
        ],
        out_specs=pl.BlockSpec((1, tt, tc), lambda b, j, k: (b, k, j)),
        out_shape=jax.ShapeDtypeStruct((B, L, C), jnp.bfloat16),
        compiler_params=_params("parallel", "parallel", "arbitrary"),
        name="inv_dft",
    )(cmat, smat, yr, z2, yn, x0, u, skip.reshape(1, C))


ROW_SPLIT_WIDTH = LANES


def _load_row_split(ref, n, rpt):
    return jnp.concatenate([ref[pl.ds(k, n, stride=rpt), :] for k in range(rpt)], axis=1)


def _branch_merge_body(ya_ref, yb_ref, ga_ref, gb_ref, wa_ref, wh_ref, m_ref):
    pa = _bdot(ya_ref[...], wa_ref[...])
    ph = _bdot(yb_ref[...], wh_ref[...])
    merged = jax.nn.sigmoid(ga_ref[...]) * pa + jax.nn.sigmoid(gb_ref[...]) * ph
    m_ref[...] = merged.astype(m_ref.dtype)


def _branch_merge(ya, yb, proj, wa, wh, tm=512):
    T = ya.shape[0]
    D = wa.shape[1]
    const = lambda i: (0, 0)
    row = lambda i: (i, 0)
    one = pl.Buffered(1)
    ga_blk = G_OFF // D
    return pl.pallas_call(
        _branch_merge_body,
        grid=(T // tm,),
        in_specs=[
            pl.BlockSpec((tm, D_CONV), row),
            pl.BlockSpec((tm, D_HYENA), row),
            pl.BlockSpec((tm, D), lambda i: (i, ga_blk)),
            pl.BlockSpec((tm, D), lambda i: (i, ga_blk + 1)),
            pl.BlockSpec((D_CONV, D), const, pipeline_mode=one),
            pl.BlockSpec((D_HYENA, D), const, pipeline_mode=one),
        ],
        out_specs=pl.BlockSpec((tm, D), row),
        out_shape=jax.ShapeDtypeStruct((T, D), jnp.bfloat16),
        compiler_params=_params("parallel"),
        name="branch_merge",
    )(ya, yb, proj, proj, wa, wh)


def _merge_body(m_ref, x_ref, wo_ref, gn_ref, wrt_ref, h_ref, xn_ref, aff_ref):
    h = x_ref[...] + _bdot(m_ref[...], wo_ref[...])
    h_ref[...] = h
    xn = _rms(h, gn_ref[...])
    xn_ref[...] = xn
    lg = lax.dot_general(wrt_ref[...], xn, (((1,), (1,)), ((), ())),
                         preferred_element_type=jnp.float32, precision=_HP)
    ex = jnp.exp(lg - jnp.max(lg, axis=0, keepdims=True))
    aff_ref[...] = ex / jnp.sum(ex, axis=0, keepdims=True)


def _merge_out(merged, x, wo, g_ffn, w_router_t, tm=512):
    T, D = x.shape
    const = lambda i: (0, 0)
    row = lambda i: (i, 0)
    one = pl.Buffered(1)
    return pl.pallas_call(
        _merge_body,
        grid=(T // tm,),
        in_specs=[
            pl.BlockSpec((tm, D), row),
            pl.BlockSpec((tm, D), row),
            pl.BlockSpec((D, D), const, pipeline_mode=one),
            pl.BlockSpec((1, D), const, pipeline_mode=one),
            pl.BlockSpec((N_EXPERTS, D), const, pipeline_mode=one),
        ],
        out_specs=[
            pl.BlockSpec((tm, D), row),
            pl.BlockSpec((tm, D), row),
            pl.BlockSpec((N_EXPERTS, tm), lambda i: (0, i)),
        ],
        out_shape=[
            jax.ShapeDtypeStruct((T, D), jnp.float32),
            jax.ShapeDtypeStruct((T, D), jnp.float32),
            jax.ShapeDtypeStruct((N_EXPERTS, T), jnp.float32),
        ],
        compiler_params=_params("parallel"),
        name="merge_out",
    )(merged, x, wo, g_ffn.reshape(1, D), w_router_t)


def _route_body(cap, aff_ref, idx_ref, g_ref, starts_ref):
    f32, bf16 = jnp.float32, jnp.bfloat16
    x = aff_ref[0]
    R = x.shape[0]
    bits = pltpu.bitcast(x, jnp.int32)

    def count(m):
        return jnp.sum(jnp.where(m, 1.0, 0.0), keepdims=True)

    def bit_step(i, cur):
        cand = cur | jnp.left_shift(jnp.int32(1), 30 - i)
        return jnp.where(count(bits >= cand) >= cap, cand, cur)

    thr = lax.fori_loop(0, 31, bit_step, jnp.zeros((1, 1), jnp.int32))

    def tri(n, cmp):
        a = lax.broadcasted_iota(jnp.int32, (n, n), 0)
        b = lax.broadcasted_iota(jnp.int32, (n, n), 1)
        return jnp.where(cmp(a, b), 1.0, 0.0).astype(bf16)

    upper = tri(LANES, lambda k, l: k <= l)
    ones = jnp.ones((LANES, LANES), bf16)
    strict_lower = tri(R, lambda r, k: k < r)

    def prefix(m01):
        mb = m01.astype(bf16)
        rowcum = _bdot(mb, upper)
        rowtot = _bdot(mb, ones)
        before = _bdot(strict_lower, rowtot.astype(bf16))
        return mb, rowcum, rowtot, before

    gt = jnp.where(bits > thr, 1.0, 0.0)
    eq = jnp.where(bits == thr, 1.0, 0.0)
    _, eq_cum, _, eq_before = prefix(eq)
    need = cap - jnp.sum(gt, keepdims=True)
    sel = gt + eq * jnp.where(eq_cum + eq_before <= need, 1.0, 0.0)
    selb, rowcum, rowtot, before = prefix(sel)

    through_row = (before + rowtot)[:, 0:1]
    j = lax.broadcasted_iota(jnp.int32, (1, cap), 1).astype(f32)
    done = through_row <= j
    r_j = jnp.sum(jnp.where(done, 1.0, 0.0), axis=0, keepdims=True)
    before_j = jnp.sum(jnp.where(done, rowtot[:, 0:1], 0.0), axis=0, keepdims=True)
    rows = lax.broadcasted_iota(jnp.int32, (R, cap), 0).astype(f32)
    pick_row = jnp.where(rows == r_j, 1.0, 0.0).astype(bf16)

    nt = (((1,), (1,)), ((), ()))
    lower = tri(LANES, lambda l, k: k <= l)
    rowcum_t = lax.dot_general(lower, selb, nt, preferred_element_type=f32)
    cum_j = _bdot(rowcum_t.astype(bf16), pick_row)
    lane_j = jnp.sum(jnp.where(cum_j <= j - before_j, 1.0, 0.0), axis=0, keepdims=True)
    idx_ref[0] = (r_j * LANES + lane_j).astype(jnp.int32)

    rowtot_t = lax.dot_general(jnp.ones((SUBLANES, LANES), bf16), selb, nt, preferred_element_type=f32)
    strict_upper = tri(R, lambda k, r: k < r)
    starts_ref[0] = _bdot(rowtot_t.astype(bf16), strict_upper).astype(jnp.int32)

    eye = tri(LANES, lambda a, b: a == b)
    hi = x.astype(bf16)
    r1 = x - hi.astype(f32)
    mid = r1.astype(bf16)
    lo = (r1 - mid.astype(f32)).astype(bf16)

    def pick(part):
        part_t = lax.dot_general(eye, part, nt, preferred_element_type=f32)
        return _bdot(part_t.astype(bf16), pick_row)

    aff_j = (pick(hi) + pick(mid)) + pick(lo)
    lanes = lax.broadcasted_iota(jnp.int32, (LANES, cap), 0).astype(f32)
    g_ref[0] = jnp.sum(jnp.where(lanes == lane_j, aff_j, 0.0), axis=0, keepdims=True)


def _route(aff_t, cap):
    E, T = aff_t.shape
    R = T // LANES
    return pl.pallas_call(
        functools.partial(_route_body, cap),
        grid=(E,),
        in_specs=[pl.BlockSpec((1, R, LANES), lambda e: (e, 0, 0))],
        out_specs=[pl.BlockSpec((1, 1, cap), lambda e: (e, 0, 0)),
                   pl.BlockSpec((1, 1, cap), lambda e: (e, 0, 0)),
                   pl.BlockSpec((1, SUBLANES, R), lambda e: (e, 0, 0))],
        out_shape=[jax.ShapeDtypeStruct((E, 1, cap), jnp.int32),
                   jax.ShapeDtypeStruct((E, 1, cap), jnp.float32),
                   jax.ShapeDtypeStruct((E, SUBLANES, R), jnp.int32)],
        compiler_params=_params("parallel"),
        name="route",
    )(aff_t.reshape(E, R, LANES))


_SEM_X, _SEM_OUT, _SEM_IDX = range(3)


def _moe_body(tm, rpt, n_tiles, n_f, idx_hbm, g_ref, wg_ref, wu_ref, wd_ref, xn_hbm, y_hbm,
              idx_smem, xrows, yrows, xb, acc, sem):
    n = pl.program_id(0) * pl.num_programs(1) + pl.program_id(1)
    f = pl.program_id(2)
    last = n_f - 1
    share = xrows.shape[0] // n_f
    next_tile = jnp.minimum(n + 1, n_tiles - 1)

    def load_idx(tile):
        cp = pltpu.make_async_copy(idx_hbm.at[tile], idx_smem, sem.at[_SEM_IDX])
        cp.start()
        cp.wait()

    def start_row(j):
        t = idx_smem[jnp.minimum(j, tm - 1)]
        pltpu.make_async_copy(xn_hbm.at[pl.ds(t, 1), :], xrows.at[pl.ds(j, 1), :], sem.at[_SEM_X]).start()

    x_landed = pltpu.make_async_copy(xn_hbm.at[pl.ds(0, xrows.shape[0]), :], xrows, sem.at[_SEM_X])
    tile_slots = pl.ds(pl.multiple_of(n * tm, tm), tm)
    y_out = pltpu.make_async_copy(y_hbm.at[pl.ds(0, tm)], y_hbm.at[tile_slots], sem.at[_SEM_OUT])

    def y_part(k):
        return pltpu.make_async_copy(yrows.at[:, pl.ds(k * LANES, LANES)], y_hbm.at[tile_slots, k, :],
                                     sem.at[_SEM_OUT])

    @pl.when(f == 0)
    def _():
        @pl.when(n == 0)
        def _():
            load_idx(0)

            def body(j, c):
                start_row(j)
                return c
            lax.fori_loop(0, xrows.shape[0], body, 0, unroll=8)

        x_landed.wait()
        xb[...] = xrows[pl.ds(0, tm), :].astype(jnp.bfloat16)
        acc[...] = jnp.zeros_like(acc)
        load_idx(next_tile)

    for u in range(share):
        start_row(f * share + u)

    x = xb[...]
    a = _bdot(x, wg_ref[0].astype(jnp.bfloat16))
    b = _bdot(x, wu_ref[0].astype(jnp.bfloat16))
    hmid = (a * jax.nn.sigmoid(a) * b).astype(jnp.bfloat16)
    acc[...] += _bdot(hmid, wd_ref[0].astype(jnp.bfloat16))

    @pl.when(f == last)
    def _():
        @pl.when(n > 0)
        def _():
            y_out.wait()

        yrows[...] = acc[...] * g_ref[0]
        for k in range(rpt):
            y_part(k).start()

        @pl.when(n == n_tiles - 1)
        def _():
            y_out.wait()
            x_landed.wait()


def _moe(idx, g, xn, w_gate, w_up, w_down, tf=256):
    E, _, C = idx.shape
    D, F = w_gate.shape[1], w_gate.shape[2]
    rpt = D // ROW_SPLIT_WIDTH
    tm = min(C, 1024)
    any_spec = pl.BlockSpec(memory_space=pl.ANY)
    n_tiles = E * (C // tm)
    n_f = F // tf
    gather_rows = -(-tm // (n_f * SUBLANES)) * SUBLANES * n_f
    return pl.pallas_call(
        functools.partial(_moe_body, tm, rpt, n_tiles, n_f),
        grid=(E, C // tm, n_f),
        in_specs=[
            any_spec,
            pl.BlockSpec((1, tm, 1), lambda e, i, f: (e, i, 0)),
            pl.BlockSpec((1, D, tf), lambda e, i, f: (e, 0, f)),
            pl.BlockSpec((1, D, tf), lambda e, i, f: (e, 0, f)),
            pl.BlockSpec((1, tf, D), lambda e, i, f: (e, f, 0)),
            any_spec,
        ],
        out_specs=any_spec,
        out_shape=jax.ShapeDtypeStruct((E * C, rpt, ROW_SPLIT_WIDTH), jnp.float32),
        scratch_shapes=[
            pltpu.SMEM((tm,), jnp.int32),
            pltpu.VMEM((gather_rows, D), jnp.float32),
            pltpu.VMEM((tm, D), jnp.float32),
            pltpu.VMEM((tm, D), jnp.bfloat16),
            pltpu.VMEM((tm, D), jnp.float32),
            pltpu.SemaphoreType.DMA((3,)),
        ],
        compiler_params=pltpu.CompilerParams(
            dimension_semantics=("arbitrary", "arbitrary", "arbitrary"),
            vmem_limit_bytes=VMEM_LIMIT_BYTES, has_side_effects=True),
        name="moe",
    )(idx.reshape(n_tiles, tm), g.reshape(E, C, 1), w_gate, w_up, w_down, xn)


_COMBINE_UNROLL = 4


def _combine_body(tt, rpt, cap, n_exp, n_rows, ch, idx_ref, starts_ref, h_ref, y_hbm, gn_ref, o_ref,
                  ybuf, acc, sem):
    i = pl.program_id(0)
    slot = i % 2
    t0 = i * tt
    total = n_exp * cap
    rows_per_tile = tt // LANES

    def lower_bound(e, row):
        s = starts_ref[e * n_rows + jnp.minimum(row, n_rows - 1)]
        return jnp.where(row < n_rows, s, cap)

    def run_of(e, tile):
        a = lower_bound(e, tile * rows_per_tile)
        b = lower_bound(e, (tile + 1) * rows_per_tile)
        return e * cap + a, b - a

    def chunk_copy(e, first_slot, s):
        start = jnp.minimum(first_slot, total - ch)
        src = y_hbm.at[pl.ds(pl.multiple_of(start * rpt, rpt), ch * rpt), :]
        return pltpu.make_async_copy(src, ybuf.at[s, e], sem.at[s, e]), first_slot - start

    def fetch_first_chunks(tile, s):
        for e in range(n_exp):
            chunk_copy(e, run_of(e, tile)[0], s)[0].start()

    @pl.when(i == 0)
    def _():
        fetch_first_chunks(0, 0)

    @pl.when(i + 1 < pl.num_programs(0))
    def _():
        fetch_first_chunks(i + 1, 1 - slot)

    acc[...] = jnp.zeros_like(acc)

    def add_rows(e, first_slot, off, m):
        def token_rows(r):
            t = idx_ref[first_slot + r] - t0
            return pl.ds(pl.multiple_of(t * rpt, rpt), rpt)

        def y_rows(r):
            return pl.ds(pl.multiple_of((off + r) * rpt, rpt), rpt)

        def group(q, c):
            r0 = q * _COMBINE_UNROLL
            dst = [token_rows(r0 + u) for u in range(_COMBINE_UNROLL)]
            sums = [acc[dst[u], :] + ybuf[slot, e, y_rows(r0 + u), :] for u in range(_COMBINE_UNROLL)]
            for u in range(_COMBINE_UNROLL):
                acc[dst[u], :] = sums[u]
            return c

        full = m // _COMBINE_UNROLL
        lax.fori_loop(0, full, group, 0)

        def single(r, c):
            d = token_rows(r)
            acc[d, :] = acc[d, :] + ybuf[slot, e, y_rows(r), :]
            return c

        lax.fori_loop(full * _COMBINE_UNROLL, m, single, 0)

    for e in range(n_exp):
        first_slot, count = run_of(e, i)
        cp, off = chunk_copy(e, first_slot, slot)
        cp.wait()
        m0 = jnp.minimum(count, ch - off)
        add_rows(e, first_slot, off, m0)

        def more(done):
            cp2, off2 = chunk_copy(e, first_slot + done, slot)
            cp2.start()
            cp2.wait()
            m = jnp.minimum(count - done, ch - off2)
            add_rows(e, first_slot + done, off2, m)
            return done + m

        lax.while_loop(lambda done: done < count, more, m0)

    o_ref[...] = _rms(h_ref[...] + _load_row_split(acc, tt, rpt), gn_ref[...])


def _combine_norm(idx, starts, h, y_rs, g_final, tt=256, ch=64):
    E, _, C = idx.shape
    T, D = h.shape
    rpt = D // ROW_SPLIT_WIDTH
    n_rows = T // LANES
    grid_spec = pltpu.PrefetchScalarGridSpec(
        num_scalar_prefetch=2,
        grid=(T // tt,),
        in_specs=[
            pl.BlockSpec((tt, D), lambda i, *_: (i, 0)),
            pl.BlockSpec(memory_space=pl.ANY),
            pl.BlockSpec((1, D), lambda i, *_: (0, 0)),
        ],
        out_specs=pl.BlockSpec((tt, D), lambda i, *_: (i, 0)),
        scratch_shapes=[
            pltpu.VMEM((2, E, ch * rpt, ROW_SPLIT_WIDTH), jnp.float32),
            pltpu.VMEM((tt * rpt, ROW_SPLIT_WIDTH), jnp.float32),
            pltpu.SemaphoreType.DMA((2, E)),
        ],
    )
    return pl.pallas_call(
        functools.partial(_combine_body, tt, rpt, C, E, n_rows, ch),
        grid_spec=grid_spec,
        out_shape=jax.ShapeDtypeStruct((T, D), jnp.float32),
        compiler_params=_params("arbitrary"),
        name="combine_norm",
    )(idx.reshape(E * C), starts[:, 0, :].reshape(E * n_rows), h,
      y_rs.reshape(E * C * rpt, ROW_SPLIT_WIDTH), g_final.reshape(1, D))


def _trunk(x, p):
    B, L, D = x.shape
    T = B * L
    xf = x.reshape(T, D)
    proj = _norm_proj(xf, p["norm_mix"], p["w_in"])
    ya, x0, u, ub = _mix_front(proj, p["conv_a_w"], p["conv_h_w"], L)

    taps = _hyena_filter_taps(L, p)
    P, Q, KN = _filter_spectrum(L, taps)
    tc = 1024 if L <= 2048 else 512
    shp = (B, L, D_HYENA)
    yr, z2, yn = _fwd_dft(ub.reshape(shp), P, Q, KN, tf=512, tc=tc)
    yb = _inv_dft(yr, z2, yn, x0.reshape(shp), u.reshape(shp), p["hyena_skip"], tt=512, tc=tc)

    merged = _branch_merge(ya, yb.reshape(T, D_HYENA), proj, p["w_proj_a"], p["w_proj_h"])
    h, xn, aff_t = _merge_out(merged, xf, p["w_out"], p["norm_ffn"], p["w_router"].T)

    cap = max(1, CAPACITY_FACTOR * T // N_EXPERTS)
    idx, g, starts = _route(aff_t, cap)
    y_rs = _moe(idx, g, xn, p["w_gate"], p["w_up"], p["w_down"])
    return _combine_norm(idx, starts, h, y_rs, p["norm_final"]).reshape(B, L, D)


def kernel(x_prompt, x_sample, w_in, conv_a_w, conv_h_w, filt_w1, filt_b1, filt_w2, filt_b2, filt_w3, filt_b3, filt_w_out, filt_freq, hyena_skip, w_proj_a, w_proj_h, w_out, norm_mix, norm_ffn, w_router, w_gate, w_up, w_down, norm_final):
    bf16 = jnp.bfloat16
    p = dict(
        w_in=w_in[0].astype(bf16), conv_a_w=conv_a_w[0], conv_h_w=conv_h_w[0],
        filt_w1=filt_w1[0], filt_b1=filt_b1[0], filt_w2=filt_w2[0], filt_b2=filt_b2[0],
        filt_w3=filt_w3[0], filt_b3=filt_b3[0], filt_w_out=filt_w_out[0], filt_freq=filt_freq[0],
        hyena_skip=hyena_skip[0], w_proj_a=w_proj_a[0].astype(bf16), w_proj_h=w_proj_h[0].astype(bf16),
        w_out=w_out[0].astype(bf16), norm_mix=norm_mix[0], norm_ffn=norm_ffn[0],
        w_router=w_router[0], w_gate=w_gate[0], w_up=w_up[0], w_down=w_down[0],
        norm_final=norm_final,
    )
    return (_trunk(x_prompt, p), _trunk(x_sample, p))
```

```python
import functools
import math

import numpy as np

import jax
import jax.numpy as jnp
from jax import lax
from jax.experimental import pallas as pl
from jax.experimental.pallas import tpu as pltpu

D_MODEL = 2048
D_CONV = 1024
D_HYENA = 1024
SHORT_K = 3
FILTER_EMB = 33
FILTER_ORDER = 64
DECAY_FAST = 0.3
DECAY_SLOW = 1.5
DECAY_TARGET = 1e-2
N_EXPERTS = 16
CAPACITY_FACTOR = 2
D_EXPERT = 5632
EPS = 1e-6
D_IN = 3 * D_CONV + 3 * D_HYENA + 2 * D_MODEL
H_OFF = 3 * D_CONV
G_OFF = 3 * D_CONV + 3 * D_HYENA

LANES = 128
SUBLANES = 8
VMEM_LIMIT_BYTES = 56 * 1024 * 1024

_HP = lax.Precision.HIGHEST


def _rms(x, g):
    r = lax.rsqrt(jnp.mean(x * x, axis=-1, keepdims=True) + EPS)
    return (x * r) * g


def _bdot(a, b):
    return jnp.dot(a, b, preferred_element_type=jnp.float32)


def _params(*sem):
    return pltpu.CompilerParams(dimension_semantics=sem, vmem_limit_bytes=VMEM_LIMIT_BYTES)


def _norm_proj_body(n_mix, x_ref, g_ref, w_ref, mix_ref, gate_ref, xn_ref):
    j = pl.program_id(1)

    @pl.when(j == 0)
    def _():
        xn_ref[...] = _rms(x_ref[...], g_ref[...]).astype(jnp.bfloat16)

    out = _bdot(xn_ref[...], w_ref[...])

    @pl.when(j < n_mix)
    def _():
        mix_ref[...] = out.astype(mix_ref.dtype)

    @pl.when(j >= n_mix)
    def _():
        gate_ref[...] = out


def _norm_proj(x, g, w_bf16, tm=1024, tn=1024):
    T, D = x.shape
    N = w_bf16.shape[1]
    n_mix = G_OFF // tn
    return pl.pallas_call(
        functools.partial(_norm_proj_body, n_mix),
        grid=(T // tm, N // tn),
        in_specs=[
            pl.BlockSpec((tm, D), lambda i, j: (i, 0)),
            pl.BlockSpec((1, D), lambda i, j: (0, 0)),
            pl.BlockSpec((D, tn), lambda i, j: (0, j)),
        ],
        out_specs=[
            pl.BlockSpec((tm, tn), lambda i, j: (i, jnp.minimum(j, n_mix - 1))),
            pl.BlockSpec((tm, tn), lambda i, j: (i, jnp.maximum(j - n_mix, 0))),
        ],
        out_shape=[
            jax.ShapeDtypeStruct((T, G_OFF), jnp.bfloat16),
            jax.ShapeDtypeStruct((T, N - G_OFF), jnp.float32),
        ],
        scratch_shapes=[pltpu.VMEM((tm, D), jnp.bfloat16)],
        compiler_params=_params("parallel", "arbitrary"),
        name="norm_proj",
    )(x, g.reshape(1, D), w_bf16)


def _shift_down(x, first_row):
    rows = lax.broadcasted_iota(jnp.int32, x.shape, 0)
    return jnp.where(rows == 0, first_row, pltpu.roll(x, 1, axis=0))


def _shift_up(x, last_row):
    n = x.shape[0]
    rows = lax.broadcasted_iota(jnp.int32, x.shape, 0)
    return jnp.where(rows == n - 1, last_row, pltpu.roll(x, n - 1, axis=0))


def _mix_body(tiles_per_seq, p_ref, pp_ref, pn_ref, wa_ref, wh_ref,
              ya_ref, x0_ref, u_ref, ub_ref):
    i = pl.program_id(0)
    keep_prev = jnp.where(i % tiles_per_seq == 0, 0.0, 1.0)
    keep_next = jnp.where(i % tiles_per_seq == tiles_per_seq - 1, 0.0, 1.0)
    C = D_CONV
    f32 = jnp.float32
    hl = pp_ref.shape[0] - 1

    def cur(lo, hi):
        return p_ref[:, lo:hi].astype(f32)

    def prev(lo, hi):
        return pp_ref[hl:hl + 1, lo:hi].astype(f32)

    def nxt(lo, hi):
        return pn_ref[0:1, lo:hi].astype(f32)

    q = cur(0, C) * cur(2 * C, 3 * C)
    q_prev = prev(0, C) * prev(2 * C, 3 * C) * keep_prev
    q_next = nxt(0, C) * nxt(2 * C, 3 * C) * keep_next
    conv = (_shift_down(q, q_prev) * wa_ref[0:1, :] + q * wa_ref[1:2, :]
            + _shift_up(q, q_next) * wa_ref[2:3, :])
    ya_ref[...] = (cur(C, 2 * C) * conv).astype(ya_ref.dtype)

    def hconv(j):
        lo, hi = H_OFF + j * D_HYENA, H_OFF + (j + 1) * D_HYENA
        wlo, whi = j * D_HYENA, (j + 1) * D_HYENA
        x = cur(lo, hi)
        xp = prev(lo, hi) * keep_prev
        xn = nxt(lo, hi) * keep_next
        return (_shift_down(x, xp) * wh_ref[0:1, wlo:whi] + x * wh_ref[1:2, wlo:whi]
                + _shift_up(x, xn) * wh_ref[2:3, wlo:whi])

    x0_ref[...] = hconv(0)
    u = hconv(2) * hconv(1)
    u_ref[...] = u
    ub_ref[...] = u.astype(ub_ref.dtype)


def _mix_front(proj, conv_a_w, conv_h_w, L, tl=256):
    T, W = proj.shape
    halo = 2 * SUBLANES
    nb = tl // halo
    last_blk = T // halo - 1
    return pl.pallas_call(
        functools.partial(_mix_body, L // tl),
        grid=(T // tl,),
        in_specs=[
            pl.BlockSpec((tl, W), lambda i: (i, 0)),
            pl.BlockSpec((halo, W), lambda i: (jnp.maximum(i * nb - 1, 0), 0)),
            pl.BlockSpec((halo, W), lambda i: (jnp.minimum((i + 1) * nb, last_blk), 0)),
            pl.BlockSpec((SHORT_K, D_CONV), lambda i: (0, 0)),
            pl.BlockSpec((SHORT_K, 3 * D_HYENA), lambda i: (0, 0)),
        ],
        out_specs=[
            pl.BlockSpec((tl, D_CONV), lambda i: (i, 0)),
            pl.BlockSpec((tl, D_HYENA), lambda i: (i, 0)),
            pl.BlockSpec((tl, D_HYENA), lambda i: (i, 0)),
            pl.BlockSpec((tl, D_HYENA), lambda i: (i, 0)),
        ],
        out_shape=[
            jax.ShapeDtypeStruct((T, D_CONV), jnp.bfloat16),
            jax.ShapeDtypeStruct((T, D_HYENA), jnp.float32),
            jax.ShapeDtypeStruct((T, D_HYENA), jnp.float32),
            jax.ShapeDtypeStruct((T, D_HYENA), jnp.bfloat16),
        ],
        compiler_params=_params("parallel"),
        name="mix_front",
    )(proj, proj, proj, conv_a_w, conv_h_w)


@functools.lru_cache(maxsize=None)
def _dft_tables(L):
    m = (np.arange(L, dtype=np.int64)[:, None] * np.arange(L, dtype=np.int64)[None, :]) % (2 * L)
    ang = m.astype(np.float64) * (math.pi / L)
    c = np.cos(ang).astype(np.float32)
    s = np.sin(ang).astype(np.float32)
    alt = np.zeros((SUBLANES, L), np.float32)
    alt[0] = 1.0 - 2.0 * (np.arange(L) % 2)
    return c.astype(jnp.bfloat16), s.astype(jnp.bfloat16), alt.astype(jnp.bfloat16)


@functools.lru_cache(maxsize=None)
def _filter_consts(L):
    t = np.linspace(0.0, 1.0, L, dtype=np.float32)[:, None]
    bands = (FILTER_EMB - 1) // 2
    ang = (np.float32(2.0 * math.pi / L) * np.arange(L, dtype=np.float32)[:, None]
           * np.linspace(1e-4, bands - 1, bands, dtype=np.float32)[None, :])
    z = np.zeros((L, LANES), np.float32)
    z[:, 0:1] = t
    z[:, 1:1 + bands] = np.cos(ang)
    z[:, 1 + bands:1 + 2 * bands] = -np.sin(ang)
    max_decay = math.log(DECAY_TARGET) / DECAY_FAST
    min_decay = math.log(DECAY_TARGET) / DECAY_SLOW
    deltas = np.abs(np.linspace(min_decay, max_decay, D_HYENA, dtype=np.float32))[None, :]
    return z, deltas


def _filter_body(z_ref, dl_ref, w1_ref, b1_ref, w2_ref, b2_ref, w3_ref, b3_ref, wo_ref, fr_ref,
                 gs_hi_ref, gs_lo_ref, gd_hi_ref, gd_lo_ref):
    z = z_ref[...]
    fr = fr_ref[...]
    h = jnp.sin(fr * (jnp.dot(z, w1_ref[...], precision=_HP, preferred_element_type=jnp.float32) + b1_ref[...]))
    h = jnp.sin(fr * (jnp.dot(h, w2_ref[...], precision=_HP, preferred_element_type=jnp.float32) + b2_ref[...]))
    h = jnp.sin(fr * (jnp.dot(h, w3_ref[...], precision=_HP, preferred_element_type=jnp.float32) + b3_ref[...]))
    h = jnp.dot(h, wo_ref[...], precision=_HP, preferred_element_type=jnp.float32)
    decay = jnp.exp(-z[:, 0:1] * dl_ref[...])
    hf = h[:, :D_HYENA] * decay
    hb = h[:, D_HYENA:] * decay
    rows = lax.broadcasted_iota(jnp.int32, hb.shape, 0) + pl.program_id(0) * hb.shape[0]
    hb = jnp.where(rows == 0, 0.0, hb)
    gs = hf + hb
    gd = hb - hf
    gs_hi = gs.astype(jnp.bfloat16)
    gd_hi = gd.astype(jnp.bfloat16)
    gs_hi_ref[...] = gs_hi
    gd_hi_ref[...] = gd_hi
    gs_lo_ref[...] = (gs - gs_hi.astype(jnp.float32)).astype(jnp.bfloat16)
    gd_lo_ref[...] = (gd - gd_hi.astype(jnp.float32)).astype(jnp.bfloat16)


def _pad_to(a, shape):
    return jnp.pad(a, [(0, n - s) for s, n in zip(a.shape, shape)])


def _hyena_filter_taps(L, p, tl=512):
    z, deltas = _filter_consts(L)
    P = LANES
    w1 = _pad_to(p["filt_w1"], (P, P))
    w2 = _pad_to(p["filt_w2"], (P, P))
    w3 = _pad_to(p["filt_w3"], (P, P))
    wo = _pad_to(p["filt_w_out"], (P, 2 * D_HYENA))
    b1 = _pad_to(p["filt_b1"].reshape(1, -1), (1, P))
    b2 = _pad_to(p["filt_b2"].reshape(1, -1), (1, P))
    b3 = _pad_to(p["filt_b3"].reshape(1, -1), (1, P))
    fr = _pad_to(p["filt_freq"].reshape(1, -1), (1, P))
    const = lambda i: (0, 0)
    out = jax.ShapeDtypeStruct((L, D_HYENA), jnp.bfloat16)
    return pl.pallas_call(
        _filter_body,
        grid=(L // tl,),
        in_specs=[
            pl.BlockSpec((tl, P), lambda i: (i, 0)),
            pl.BlockSpec((1, D_HYENA), const),
            pl.BlockSpec((P, P), const), pl.BlockSpec((1, P), const),
            pl.BlockSpec((P, P), const), pl.BlockSpec((1, P), const),
            pl.BlockSpec((P, P), const), pl.BlockSpec((1, P), const),
            pl.BlockSpec((P, 2 * D_HYENA), const), pl.BlockSpec((1, P), const),
        ],
        out_specs=[pl.BlockSpec((tl, D_HYENA), lambda i: (i, 0))] * 4,
        out_shape=[out] * 4,
        compiler_params=_params("parallel"),
        name="hyena_filter",
    )(z, deltas, w1, b1, w2, b2, w3, b3, wo, fr)


def _spectrum_body(L, c_ref, s_ref, alt_ref, gsh_ref, gsl_ref, gdh_ref, gdl_ref,
                   p_ref, q_ref, kn_ref):
    c = c_ref[...]
    s = s_ref[...]
    tf = c.shape[0]
    f = lax.broadcasted_iota(jnp.int32, (tf, 1), 0) + pl.program_id(1) * tf
    scale = jnp.where(f == 0, 1.0, 2.0) * (1.0 / (2 * L))
    p_ref[...] = (_bdot(c, gsh_ref[...]) + _bdot(c, gsl_ref[...])) * scale
    q_ref[...] = (_bdot(s, gdh_ref[...]) + _bdot(s, gdl_ref[...])) * scale
    alt = alt_ref[...]
    kn_ref[...] = (_bdot(alt, gsh_ref[...]) + _bdot(alt, gsl_ref[...])) * (1.0 / (2 * L))


def _filter_spectrum(L, taps, tf=512, tc=256):
    cmat, smat, alt = _dft_tables(L)
    gcol = pl.BlockSpec((L, tc), lambda j, k: (0, j))
    return pl.pallas_call(
        functools.partial(_spectrum_body, L),
        grid=(D_HYENA // tc, L // tf),
        in_specs=[
            pl.BlockSpec((tf, L), lambda j, k: (k, 0)),
            pl.BlockSpec((tf, L), lambda j, k: (k, 0)),
            pl.BlockSpec((SUBLANES, L), lambda j, k: (0, 0)),
            gcol, gcol, gcol, gcol,
        ],
        out_specs=[
            pl.BlockSpec((tf, tc), lambda j, k: (k, j)),
            pl.BlockSpec((tf, tc), lambda j, k: (k, j)),
            pl.BlockSpec((SUBLANES, tc), lambda j, k: (0, j)),
        ],
        out_shape=[
            jax.ShapeDtypeStruct((L, D_HYENA), jnp.float32),
            jax.ShapeDtypeStruct((L, D_HYENA), jnp.float32),
            jax.ShapeDtypeStruct((SUBLANES, D_HYENA), jnp.float32),
        ],
        compiler_params=_params("parallel", "arbitrary"),
        name="filter_spectrum",
    )(cmat, smat, alt, *taps)


def _fwd_dft_body(c_ref, s_ref, alt_ref, u_ref, p_ref, q_ref, kn_ref, yr_ref, z2_ref, yn_ref):
    u = u_ref[0]
    a = _bdot(c_ref[...], u)
    b = _bdot(s_ref[...], u)
    p = p_ref[...]
    q = q_ref[...]
    yr_ref[0] = (a * p + b * q).astype(yr_ref.dtype)
    z2_ref[0] = (b * p - a * q).astype(z2_ref.dtype)
    yn_ref[0] = _bdot(alt_ref[...], u) * kn_ref[...]


def _fwd_dft(u_bf16, P, Q, KN, tf, tc):
    B, L, C = u_bf16.shape
    cmat, smat, alt = _dft_tables(L)
    return pl.pallas_call(
        _fwd_dft_body,
        grid=(B, C // tc, L // tf),
        in_specs=[
            pl.BlockSpec((tf, L), lambda b, j, k: (k, 0)),
            pl.BlockSpec((tf, L), lambda b, j, k: (k, 0)),
            pl.BlockSpec((SUBLANES, L), lambda b, j, k: (0, 0)),
            pl.BlockSpec((1, L, tc), lambda b, j, k: (b, 0, j)),
            pl.BlockSpec((tf, tc), lambda b, j, k: (k, j)),
            pl.BlockSpec((tf, tc), lambda b, j, k: (k, j)),
            pl.BlockSpec((SUBLANES, tc), lambda b, j, k: (0, j)),
        ],
        out_specs=[
            pl.BlockSpec((1, tf, tc), lambda b, j, k: (b, k, j)),
            pl.BlockSpec((1, tf, tc), lambda b, j, k: (b, k, j)),
            pl.BlockSpec((1, SUBLANES, tc), lambda b, j, k: (b, 0, j)),
        ],
        out_shape=[
            jax.ShapeDtypeStruct((B, L, C), jnp.bfloat16),
            jax.ShapeDtypeStruct((B, L, C), jnp.bfloat16),
            jax.ShapeDtypeStruct((B, SUBLANES, C), jnp.float32),
        ],
        compiler_params=_params("parallel", "parallel", "arbitrary"),
        name="fwd_dft",
    )(cmat, smat, alt, u_bf16, P, Q, KN)


def _inv_dft_body(c_ref, s_ref, yr_ref, z2_ref, yn_ref, x0_ref, u_ref, skip_ref, o_ref):
    y = _bdot(c_ref[...], yr_ref[0]) + _bdot(s_ref[...], z2_ref[0])
    tt = y.shape[0]
    t = lax.broadcasted_iota(jnp.int32, (tt, 1), 0) + pl.program_id(2) * tt
    sign = (1 - 2 * (t & 1)).astype(jnp.float32)
    y = y + sign * yn_ref[0, 0:1, :]
    o_ref[0] = (x0_ref[0] * (y + u_ref[0] * skip_ref[...])).astype(o_ref.dtype)


def _inv_dft(yr, z2, yn, x0, u, skip, tt, tc):
    B, L, C = yr.shape
    cmat, smat, _ = _dft_tables(L)
    return pl.pallas_call(
        _inv_dft_body,
        grid=(B, C // tc, L // tt),
        in_specs=[
            pl.BlockSpec((tt, L), lambda b, j, k: (k, 0)),
            pl.BlockSpec((tt, L), lambda b, j, k: (k, 0)),
            pl.BlockSpec((1, L, tc), lambda b, j, k: (b, 0, j)),
            pl.BlockSpec((1, L, tc), lambda b, j, k: (b, 0, j)),
            pl.BlockSpec((1, SUBLANES, tc), lambda b, j, k: (b, 0, j)),
            pl.BlockSpec((1, tt, tc), lambda b, j, k: (b, k, j)),
            pl.BlockSpec((1, tt, tc), lambda b, j, k: (b, k, j)),
            pl.BlockSpec((1, tc), lambda b, j, k: (0, j)),
        ],
        out_specs=pl.BlockSpec((1, tt, tc), lambda b, j, k: (b, k, j)),
        out_shape=jax.ShapeDtypeStruct((B, L, C), jnp.bfloat16),
        compiler_params=_params("parallel", "parallel", "arbitrary"),
        name="inv_dft",
    )(cmat, smat, yr, z2, yn, x0, u, skip.reshape(1, C))


ROW_SPLIT_WIDTH = LANES


def _load_row_split(ref, n, rpt):
    return jnp.concatenate([ref[pl.ds(k, n, stride=rpt), :] for k in range(rpt)], axis=1)


def _branch_merge_body(ya_ref, yb_ref, ga_ref, gb_ref, wa_ref, wh_ref, m_ref):
    pa = _bdot(ya_ref[...], wa_ref[...])
    ph = _bdot(yb_ref[...], wh_ref[...])
    merged = jax.nn.sigmoid(ga_ref[...]) * pa + jax.nn.sigmoid(gb_ref[...]) * ph
    m_ref[...] = merged.astype(m_ref.dtype)


def _branch_merge(ya, yb, gates, wa, wh, tm=512):
    T = ya.shape[0]
    D = wa.shape[1]
    const = lambda i: (0, 0)
    row = lambda i: (i, 0)
    one = pl.Buffered(1)
    return pl.pallas_call(
        _branch_merge_body,
        grid=(T // tm,),
        in_specs=[
            pl.BlockSpec((tm, D_CONV), row),
            pl.BlockSpec((tm, D_HYENA), row),
            pl.BlockSpec((tm, D), lambda i: (i, 0)),
            pl.BlockSpec((tm, D), lambda i: (i, 1)),
            pl.BlockSpec((D_CONV, D), const, pipeline_mode=one),
            pl.BlockSpec((D_HYENA, D), const, pipeline_mode=one),
        ],
        out_specs=pl.BlockSpec((tm, D), row),
        out_shape=jax.ShapeDtypeStruct((T, D), jnp.bfloat16),
        compiler_params=_params("parallel"),
        name="branch_merge",
    )(ya, yb, gates, gates, wa, wh)


def _merge_body(m_ref, x_ref, wo_ref, gn_ref, wrt_ref, h_ref, xn_ref, aff_ref):
    h = x_ref[...] + _bdot(m_ref[...], wo_ref[...])
    h_ref[...] = h
    xn = _rms(h, gn_ref[...])
    xn_ref[...] = xn
    lg = lax.dot_general(wrt_ref[...], xn, (((1,), (1,)), ((), ())),
                         preferred_element_type=jnp.float32, precision=_HP)
    ex = jnp.exp(lg - jnp.max(lg, axis=0, keepdims=True))
    aff_ref[...] = ex / jnp.sum(ex, axis=0, keepdims=True)


def _merge_out(merged, x, wo, g_ffn, w_router_t, tm=512):
    T, D = x.shape
    const = lambda i: (0, 0)
    row = lambda i: (i, 0)
    one = pl.Buffered(1)
    return pl.pallas_call(
        _merge_body,
        grid=(T // tm,),
        in_specs=[
            pl.BlockSpec((tm, D), row),
            pl.BlockSpec((tm, D), row),
            pl.BlockSpec((D, D), const, pipeline_mode=one),
            pl.BlockSpec((1, D), const, pipeline_mode=one),
            pl.BlockSpec((N_EXPERTS, D), const, pipeline_mode=one),
        ],
        out_specs=[
            pl.BlockSpec((tm, D), row),
            pl.BlockSpec((tm, D), row),
            pl.BlockSpec((N_EXPERTS, tm), lambda i: (0, i)),
        ],
        out_shape=[
            jax.ShapeDtypeStruct((T, D), jnp.float32),
            jax.ShapeDtypeStruct((T, D), jnp.float32),
            jax.ShapeDtypeStruct((N_EXPERTS, T), jnp.float32),
        ],
        compiler_params=_params("parallel"),
        name="merge_out",
    )(merged, x, wo, g_ffn.reshape(1, D), w_router_t)


def _route_body(cap, aff_ref, idx_ref, g_ref, starts_ref):
    f32, bf16 = jnp.float32, jnp.bfloat16
    x = aff_ref[0]
    R = x.shape[0]
    bits = pltpu.bitcast(x, jnp.int32)

    def count(m):
        return jnp.sum(jnp.where(m, 1.0, 0.0), keepdims=True)

    def bit_step(i, cur):
        cand = cur | jnp.left_shift(jnp.int32(1), 30 - i)
        return jnp.where(count(bits >= cand) >= cap, cand, cur)

    thr = lax.fori_loop(0, 31, bit_step, jnp.zeros((1, 1), jnp.int32))

    def tri(n, cmp):
        a = lax.broadcasted_iota(jnp.int32, (n, n), 0)
        b = lax.broadcasted_iota(jnp.int32, (n, n), 1)
        return jnp.where(cmp(a, b), 1.0, 0.0).astype(bf16)

    upper = tri(LANES, lambda k, l: k <= l)
    ones = jnp.ones((LANES, LANES), bf16)
    strict_lower = tri(R, lambda r, k: k < r)

    def prefix(m01):
        mb = m01.astype(bf16)
        rowcum = _bdot(mb, upper)
        rowtot = _bdot(mb, ones)
        before = _bdot(strict_lower, rowtot.astype(bf16))
        return mb, rowcum, rowtot, before

    gt = jnp.where(bits > thr, 1.0, 0.0)
    eq = jnp.where(bits == thr, 1.0, 0.0)
    _, eq_cum, _, eq_before = prefix(eq)
    need = cap - jnp.sum(gt, keepdims=True)
    sel = gt + eq * jnp.where(eq_cum + eq_before <= need, 1.0, 0.0)
    selb, rowcum, rowtot, before = prefix(sel)

    through_row = (before + rowtot)[:, 0:1]
    j = lax.broadcasted_iota(jnp.int32, (1, cap), 1).astype(f32)
    done = through_row <= j
    r_j = jnp.sum(jnp.where(done, 1.0, 0.0), axis=0, keepdims=True)
    before_j = jnp.sum(jnp.where(done, rowtot[:, 0:1], 0.0), axis=0, keepdims=True)
    rows = lax.broadcasted_iota(jnp.int32, (R, cap), 0).astype(f32)
    pick_row = jnp.where(rows == r_j, 1.0, 0.0).astype(bf16)

    nt = (((1,), (1,)), ((), ()))
    lower = tri(LANES, lambda l, k: k <= l)
    rowcum_t = lax.dot_general(lower, selb, nt, preferred_element_type=f32)
    cum_j = _bdot(rowcum_t.astype(bf16), pick_row)
    lane_j = jnp.sum(jnp.where(cum_j <= j - before_j, 1.0, 0.0), axis=0, keepdims=True)
    idx_ref[0] = (r_j * LANES + lane_j).astype(jnp.int32)

    rowtot_t = lax.dot_general(jnp.ones((SUBLANES, LANES), bf16), selb, nt, preferred_element_type=f32)
    strict_upper = tri(R, lambda k, r: k < r)
    starts_ref[0] = _bdot(rowtot_t.astype(bf16), strict_upper).astype(jnp.int32)

    eye = tri(LANES, lambda a, b: a == b)
    hi = x.astype(bf16)
    r1 = x - hi.astype(f32)
    mid = r1.astype(bf16)
    lo = (r1 - mid.astype(f32)).astype(bf16)

    def pick(part):
        part_t = lax.dot_general(eye, part, nt, preferred_element_type=f32)
        return _bdot(part_t.astype(bf16), pick_row)

    aff_j = (pick(hi) + pick(mid)) + pick(lo)
    lanes = lax.broadcasted_iota(jnp.int32, (LANES, cap), 0).astype(f32)
    g_ref[0] = jnp.sum(jnp.where(lanes == lane_j, aff_j, 0.0), axis=0, keepdims=True)


def _route(aff_t, cap):
    E, T = aff_t.shape
    R = T // LANES
    return pl.pallas_call(
        functools.partial(_route_body, cap),
        grid=(E,),
        in_specs=[pl.BlockSpec((1, R, LANES), lambda e: (e, 0, 0))],
        out_specs=[pl.BlockSpec((1, 1, cap), lambda e: (e, 0, 0)),
                   pl.BlockSpec((1, 1, cap), lambda e: (e, 0, 0)),
                   pl.BlockSpec((1, SUBLANES, R), lambda e: (e, 0, 0))],
        out_shape=[jax.ShapeDtypeStruct((E, 1, cap), jnp.int32),
                   jax.ShapeDtypeStruct((E, 1, cap), jnp.float32),
                   jax.ShapeDtypeStruct((E, SUBLANES, R), jnp.int32)],
        compiler_params=_params("parallel"),
        name="route",
    )(aff_t.reshape(E, R, LANES))


_SEM_X, _SEM_OUT, _SEM_IDX = range(3)


def _moe_body(tm, rpt, n_tiles, n_f, idx_hbm, g_ref, wg_ref, wu_ref, wd_ref, xn_hbm, y_hbm,
              idx_smem, xrows, yrows, xb, acc, sem):
    n = pl.program_id(0) * pl.num_programs(1) + pl.program_id(1)
    f = pl.program_id(2)
    last = n_f - 1
    share = xrows.shape[0] // n_f
    next_tile = jnp.minimum(n + 1, n_tiles - 1)

    def load_idx(tile):
        cp = pltpu.make_async_copy(idx_hbm.at[tile], idx_smem, sem.at[_SEM_IDX])
        cp.start()
        cp.wait()

    def start_row(j):
        t = idx_smem[jnp.minimum(j, tm - 1)]
        pltpu.make_async_copy(xn_hbm.at[pl.ds(t, 1), :], xrows.at[pl.ds(j, 1), :], sem.at[_SEM_X]).start()

    x_landed = pltpu.make_async_copy(xn_hbm.at[pl.ds(0, xrows.shape[0]), :], xrows, sem.at[_SEM_X])
    tile_rows = pl.ds(pl.multiple_of(n * (tm * rpt), tm * rpt), tm * rpt)
    y_out = pltpu.make_async_copy(yrows, y_hbm.at[tile_rows, :], sem.at[_SEM_OUT])

    @pl.when(f == 0)
    def _():
        @pl.when(n == 0)
        def _():
            load_idx(0)

            def body(j, c):
                start_row(j)
                return c
            lax.fori_loop(0, xrows.shape[0], body, 0, unroll=8)

        x_landed.wait()
        xb[...] = xrows[pl.ds(0, tm), :].astype(jnp.bfloat16)
        acc[...] = jnp.zeros_like(acc)
        load_idx(next_tile)

    for u in range(share):
        start_row(f * share + u)

    x = xb[...]
    a = _bdot(x, wg_ref[0].astype(jnp.bfloat16))
    b = _bdot(x, wu_ref[0].astype(jnp.bfloat16))
    hmid = (a * jax.nn.sigmoid(a) * b).astype(jnp.bfloat16)
    acc[...] += _bdot(hmid, wd_ref[0].astype(jnp.bfloat16))

    @pl.when(f == last)
    def _():
        @pl.when(n > 0)
        def _():
            y_out.wait()

        g = g_ref[0]
        for k in range(rpt):
            yrows[pl.ds(k, tm, stride=rpt), :] = acc[:, k * LANES:(k + 1) * LANES] * g
        y_out.start()

        @pl.when(n == n_tiles - 1)
        def _():
            y_out.wait()
            x_landed.wait()


def _moe(idx, g, xn, w_gate, w_up, w_down, tf=256):
    E, _, C = idx.shape
    D, F = w_gate.shape[1], w_gate.shape[2]
    rpt = D // ROW_SPLIT_WIDTH
    tm = min(C, 1024)
    any_spec = pl.BlockSpec(memory_space=pl.ANY)
    n_tiles = E * (C // tm)
    n_f = F // tf
    gather_rows = -(-tm // (n_f * SUBLANES)) * SUBLANES * n_f
    return pl.pallas_call(
        functools.partial(_moe_body, tm, rpt, n_tiles, n_f),
        grid=(E, C // tm, n_f),
        in_specs=[
            any_spec,
            pl.BlockSpec((1, tm, 1), lambda e, i, f: (e, i, 0)),
            pl.BlockSpec((1, D, tf), lambda e, i, f: (e, 0, f)),
            pl.BlockSpec((1, D, tf), lambda e, i, f: (e, 0, f)),
            pl.BlockSpec((1, tf, D), lambda e, i, f: (e, f, 0)),
            any_spec,
        ],
        out_specs=any_spec,
        out_shape=jax.ShapeDtypeStruct((E * C * rpt, ROW_SPLIT_WIDTH), jnp.float32),
        scratch_shapes=[
            pltpu.SMEM((tm,), jnp.int32),
            pltpu.VMEM((gather_rows, D), jnp.float32),
            pltpu.VMEM((tm * rpt, ROW_SPLIT_WIDTH), jnp.float32),
            pltpu.VMEM((tm, D), jnp.bfloat16),
            pltpu.VMEM((tm, D), jnp.float32),
            pltpu.SemaphoreType.DMA((3,)),
        ],
        compiler_params=pltpu.CompilerParams(
            dimension_semantics=("arbitrary", "arbitrary", "arbitrary"),
            vmem_limit_bytes=VMEM_LIMIT_BYTES, has_side_effects=True),
        name="moe",
    )(idx.reshape(n_tiles, tm), g.reshape(E, C, 1), w_gate, w_up, w_down, xn)


_COMBINE_UNROLL = 4


def _combine_body(tt, rpt, cap, n_exp, n_rows, ch, idx_ref, starts_ref, h_ref, y_hbm, gn_ref, o_ref,
                  ybuf, acc, sem):
    i = pl.program_id(0)
    slot = i % 2
    t0 = i * tt
    total = n_exp * cap
    rows_per_tile = tt // LANES

    def lower_bound(e, row):
        s = starts_ref[e * n_rows + jnp.minimum(row, n_rows - 1)]
        return jnp.where(row < n_rows, s, cap)

    def run_of(e, tile):
        a = lower_bound(e, tile * rows_per_tile)
        b = lower_bound(e, (tile + 1) * rows_per_tile)
        return e * cap + a, b - a

    def chunk_copy(e, first_slot, s):
        start = jnp.minimum(first_slot, total - ch)
        src = y_hbm.at[pl.ds(pl.multiple_of(start * rpt, rpt), ch * rpt), :]
        return pltpu.make_async_copy(src, ybuf.at[s, e], sem.at[s, e]), first_slot - start

    def fetch_first_chunks(tile, s):
        for e in range(n_exp):
            chunk_copy(e, run_of(e, tile)[0], s)[0].start()

    @pl.when(i == 0)
    def _():
        fetch_first_chunks(0, 0)

    @pl.when(i + 1 < pl.num_programs(0))
    def _():
        fetch_first_chunks(i + 1, 1 - slot)

    acc[...] = jnp.zeros_like(acc)

    def add_rows(e, first_slot, off, m):
        def token_rows(r):
            t = idx_ref[first_slot + r] - t0
            return pl.ds(pl.multiple_of(t * rpt, rpt), rpt)

        def y_rows(r):
            return pl.ds(pl.multiple_of((off + r) * rpt, rpt), rpt)

        def group(q, c):
            r0 = q * _COMBINE_UNROLL
            dst = [token_rows(r0 + u) for u in range(_COMBINE_UNROLL)]
            sums = [acc[dst[u], :] + ybuf[slot, e, y_rows(r0 + u), :] for u in range(_COMBINE_UNROLL)]
            for u in range(_COMBINE_UNROLL):
                acc[dst[u], :] = sums[u]
            return c

        full = m // _COMBINE_UNROLL
        lax.fori_loop(0, full, group, 0)

        def single(r, c):
            d = token_rows(r)
            acc[d, :] = acc[d, :] + ybuf[slot, e, y_rows(r), :]
            return c

        lax.fori_loop(full * _COMBINE_UNROLL, m, single, 0)

    for e in range(n_exp):
        first_slot, count = run_of(e, i)
        cp, off = chunk_copy(e, first_slot, slot)
        cp.wait()
        m0 = jnp.minimum(count, ch - off)
        add_rows(e, first_slot, off, m0)

        def more(done):
            cp2, off2 = chunk_copy(e, first_slot + done, slot)
            cp2.start()
            cp2.wait()
            m = jnp.minimum(count - done, ch - off2)
            add_rows(e, first_slot + done, off2, m)
            return done + m

        lax.while_loop(lambda done: done < count, more, m0)

    o_ref[...] = _rms(h_ref[...] + _load_row_split(acc, tt, rpt), gn_ref[...])


def _combine_norm(idx, starts, h, y_rs, g_final, tt=256, ch=64):
    E, _, C = idx.shape
    T, D = h.shape
    rpt = D // ROW_SPLIT_WIDTH
    n_rows = T // LANES
    grid_spec = pltpu.PrefetchScalarGridSpec(
        num_scalar_prefetch=2,
        grid=(T // tt,),
        in_specs=[
            pl.BlockSpec((tt, D), lambda i, *_: (i, 0)),
            pl.BlockSpec(memory_space=pl.ANY),
            pl.BlockSpec((1, D), lambda i, *_: (0, 0)),
        ],
        out_specs=pl.BlockSpec((tt, D), lambda i, *_: (i, 0)),
        scratch_shapes=[
            pltpu.VMEM((2, E, ch * rpt, ROW_SPLIT_WIDTH), jnp.float32),
            pltpu.VMEM((tt * rpt, ROW_SPLIT_WIDTH), jnp.float32),
            pltpu.SemaphoreType.DMA((2, E)),
        ],
    )
    return pl.pallas_call(
        functools.partial(_combine_body, tt, rpt, C, E, n_rows, ch),
        grid_spec=grid_spec,
        out_shape=jax.ShapeDtypeStruct((T, D), jnp.float32),
        compiler_params=_params("arbitrary"),
        name="combine_norm",
    )(idx.reshape(E * C), starts[:, 0, :].reshape(E * n_rows), h, y_rs, g_final.reshape(1, D))


def _trunk(x, p):
    B, L, D = x.shape
    T = B * L
    xf = x.reshape(T, D)
    mix_in, gates = _norm_proj(xf, p["norm_mix"], p["w_in"])
    ya, x0, u, ub = _mix_front(mix_in, p["conv_a_w"], p["conv_h_w"], L)

    taps = _hyena_filter_taps(L, p)
    P, Q, KN = _filter_spectrum(L, taps)
    tc = 1024 if L <= 2048 else 512
    shp = (B, L, D_HYENA)
    yr, z2, yn = _fwd_dft(ub.reshape(shp), P, Q, KN, tf=512, tc=tc)
    yb = _inv_dft(yr, z2, yn, x0.reshape(shp), u.reshape(shp), p["hyena_skip"], tt=512, tc=tc)

    merged = _branch_merge(ya, yb.reshape(T, D_HYENA), gates, p["w_proj_a"], p["w_proj_h"])
    h, xn, aff_t = _merge_out(merged, xf, p["w_out"], p["norm_ffn"], p["w_router"].T)

    cap = max(1, CAPACITY_FACTOR * T // N_EXPERTS)
    idx, g, starts = _route(aff_t, cap)
    y_rs = _moe(idx, g, xn, p["w_gate"], p["w_up"], p["w_down"])
    return _combine_norm(idx, starts, h, y_rs, p["norm_final"]).reshape(B, L, D)


def kernel(x_prompt, x_sample, w_in, conv_a_w, conv_h_w, filt_w1, filt_b1, filt_w2, filt_b2, filt_w3, filt_b3, filt_w_out, filt_freq, hyena_skip, w_proj_a, w_proj_h, w_out, norm_mix, norm_ffn, w_router, w_gate, w_up, w_down, norm_final):
    bf16 = jnp.bfloat16
    p = dict(
        w_in=w_in[0].astype(bf16), conv_a_w=conv_a_w[0], conv_h_w=conv_h_w[0],
        filt_w1=filt_w1[0], filt_b1=filt_b1[0], filt_w2=filt_w2[0], filt_b2=filt_b2[0],
        filt_w3=filt_w3[0], filt_b3=filt_b3[0], filt_w_out=filt_w_out[0], filt_freq=filt_freq[0],
        hyena_skip=hyena_skip[0], w_proj_a=w_proj_a[0].astype(bf16), w_proj_h=w_proj_h[0].astype(bf16),
        w_out=w_out[0].astype(bf16), norm_mix=norm_mix[0], norm_ffn=norm_ffn[0],
        w_router=w_router[0], w_gate=w_gate[0], w_up=w_up[0], w_down=w_down[0],
        norm_final=norm_final,
    )
    return (_trunk(x_prompt, p), _trunk(x_sample, p))
```

```python
import functools
import math

import numpy as np

import jax
import jax.numpy as jnp
from jax import lax
from jax.experimental import pallas as pl
from jax.experimental.pallas import tpu as pltpu

D_MODEL = 2048
D_CONV = 1024
D_HYENA = 1024
SHORT_K = 3
FILTER_EMB = 33
FILTER_ORDER = 64
DECAY_FAST = 0.3
DECAY_SLOW = 1.5
DECAY_TARGET = 1e-2
N_EXPERTS = 16
CAPACITY_FACTOR = 2
D_EXPERT = 5632
EPS = 1e-6
D_IN = 3 * D_CONV + 3 * D_HYENA + 2 * D_MODEL
H_OFF = 3 * D_CONV
G_OFF = 3 * D_CONV + 3 * D_HYENA

LANES = 128
SUBLANES = 8
VMEM_LIMIT_BYTES = 56 * 1024 * 1024

_HP = lax.Precision.HIGHEST


def _rms(x, g):
    r = lax.rsqrt(jnp.mean(x * x, axis=-1, keepdims=True) + EPS)
    return (x * r) * g


def _bdot(a, b):
    return jnp.dot(a, b, preferred_element_type=jnp.float32)


def _params(*sem):
    return pltpu.CompilerParams(dimension_semantics=sem, vmem_limit_bytes=VMEM_LIMIT_BYTES)


def _norm_proj_body(x_ref, g_ref, w_ref, o_ref, xn_ref):
    @pl.when(pl.program_id(1) == 0)
    def _():
        xn_ref[...] = _rms(x_ref[...], g_ref[...]).astype(jnp.bfloat16)

    o_ref[...] = _bdot(xn_ref[...], w_ref[...])


def _norm_proj(x, g, w_bf16, tm=1024, tn=2048):
    T, D = x.shape
    N = w_bf16.shape[1]
    return pl.pallas_call(
        _norm_proj_body,
        grid=(T // tm, N // tn),
        in_specs=[
            pl.BlockSpec((tm, D), lambda i, j: (i, 0)),
            pl.BlockSpec((1, D), lambda i, j: (0, 0)),
            pl.BlockSpec((D, tn), lambda i, j: (0, j)),
        ],
        out_specs=pl.BlockSpec((tm, tn), lambda i, j: (i, j)),
        out_shape=jax.ShapeDtypeStruct((T, N), jnp.float32),
        scratch_shapes=[pltpu.VMEM((tm, D), jnp.bfloat16)],
        compiler_params=_params("parallel", "arbitrary"),
        name="norm_proj",
    )(x, g.reshape(1, D), w_bf16)


def _shift_down(x, first_row):
    rows = lax.broadcasted_iota(jnp.int32, x.shape, 0)
    return jnp.where(rows == 0, first_row, pltpu.roll(x, 1, axis=0))


def _shift_up(x, last_row):
    n = x.shape[0]
    rows = lax.broadcasted_iota(jnp.int32, x.shape, 0)
    return jnp.where(rows == n - 1, last_row, pltpu.roll(x, n - 1, axis=0))


def _mix_body(tiles_per_seq, p_ref, pp_ref, pn_ref, wa_ref, wh_ref,
              ya_ref, x0_ref, u_ref, ub_ref):
    i = pl.program_id(0)
    keep_prev = jnp.where(i % tiles_per_seq == 0, 0.0, 1.0)
    keep_next = jnp.where(i % tiles_per_seq == tiles_per_seq - 1, 0.0, 1.0)
    C = D_CONV
    q = p_ref[:, 0:C] * p_ref[:, 2 * C:3 * C]
    q_prev = pp_ref[SUBLANES - 1:SUBLANES, 0:C] * pp_ref[SUBLANES - 1:SUBLANES, 2 * C:3 * C] * keep_prev
    q_next = pn_ref[0:1, 0:C] * pn_ref[0:1, 2 * C:3 * C] * keep_next
    conv = (_shift_down(q, q_prev) * wa_ref[0:1, :] + q * wa_ref[1:2, :]
            + _shift_up(q, q_next) * wa_ref[2:3, :])
    ya_ref[...] = (p_ref[:, C:2 * C] * conv).astype(ya_ref.dtype)

    def hconv(j):
        lo, hi = H_OFF + j * D_HYENA, H_OFF + (j + 1) * D_HYENA
        wlo, whi = j * D_HYENA, (j + 1) * D_HYENA
        x = p_ref[:, lo:hi]
        xp = pp_ref[SUBLANES - 1:SUBLANES, lo:hi] * keep_prev
        xn = pn_ref[0:1, lo:hi] * keep_next
        return (_shift_down(x, xp) * wh_ref[0:1, wlo:whi] + x * wh_ref[1:2, wlo:whi]
                + _shift_up(x, xn) * wh_ref[2:3, wlo:whi])

    x0_ref[...] = hconv(0)
    u = hconv(2) * hconv(1)
    u_ref[...] = u
    ub_ref[...] = u.astype(ub_ref.dtype)


def _mix_front(proj, conv_a_w, conv_h_w, L, tl=256):
    T = proj.shape[0]
    W = G_OFF
    nb = tl // SUBLANES
    last_blk = T // SUBLANES - 1
    return pl.pallas_call(
        functools.partial(_mix_body, L // tl),
        grid=(T // tl,),
        in_specs=[
            pl.BlockSpec((tl, W), lambda i: (i, 0)),
            pl.BlockSpec((SUBLANES, W), lambda i: (jnp.maximum(i * nb - 1, 0), 0)),
            pl.BlockSpec((SUBLANES, W), lambda i: (jnp.minimum((i + 1) * nb, last_blk), 0)),
            pl.BlockSpec((SHORT_K, D_CONV), lambda i: (0, 0)),
            pl.BlockSpec((SHORT_K, 3 * D_HYENA), lambda i: (0, 0)),
        ],
        out_specs=[
            pl.BlockSpec((tl, D_CONV), lambda i: (i, 0)),
            pl.BlockSpec((tl, D_HYENA), lambda i: (i, 0)),
            pl.BlockSpec((tl, D_HYENA), lambda i: (i, 0)),
            pl.BlockSpec((tl, D_HYENA), lambda i: (i, 0)),
        ],
        out_shape=[
            jax.ShapeDtypeStruct((T, D_CONV), jnp.bfloat16),
            jax.ShapeDtypeStruct((T, D_HYENA), jnp.float32),
            jax.ShapeDtypeStruct((T, D_HYENA), jnp.float32),
            jax.ShapeDtypeStruct((T, D_HYENA), jnp.bfloat16),
        ],
        compiler_params=_params("parallel"),
        name="mix_front",
    )(proj, proj, proj, conv_a_w, conv_h_w)


@functools.lru_cache(maxsize=None)
def _dft_tables(L):
    m = (np.arange(L, dtype=np.int64)[:, None] * np.arange(L, dtype=np.int64)[None, :]) % (2 * L)
    ang = m.astype(np.float64) * (math.pi / L)
    c = np.cos(ang).astype(np.float32)
    s = np.sin(ang).astype(np.float32)
    alt = np.zeros((SUBLANES, L), np.float32)
    alt[0] = 1.0 - 2.0 * (np.arange(L) % 2)
    return c.astype(jnp.bfloat16), s.astype(jnp.bfloat16), alt.astype(jnp.bfloat16)


@functools.lru_cache(maxsize=None)
def _filter_consts(L):
    t = np.linspace(0.0, 1.0, L, dtype=np.float32)[:, None]
    bands = (FILTER_EMB - 1) // 2
    ang = (np.float32(2.0 * math.pi / L) * np.arange(L, dtype=np.float32)[:, None]
           * np.linspace(1e-4, bands - 1, bands, dtype=np.float32)[None, :])
    z = np.zeros((L, LANES), np.float32)
    z[:, 0:1] = t
    z[:, 1:1 + bands] = np.cos(ang)
    z[:, 1 + bands:1 + 2 * bands] = -np.sin(ang)
    max_decay = math.log(DECAY_TARGET) / DECAY_FAST
    min_decay = math.log(DECAY_TARGET) / DECAY_SLOW
    deltas = np.abs(np.linspace(min_decay, max_decay, D_HYENA, dtype=np.float32))[None, :]
    return z, deltas


def _filter_body(z_ref, dl_ref, w1_ref, b1_ref, w2_ref, b2_ref, w3_ref, b3_ref, wo_ref, fr_ref,
                 gs_hi_ref, gs_lo_ref, gd_hi_ref, gd_lo_ref):
    z = z_ref[...]
    fr = fr_ref[...]
    h = jnp.sin(fr * (jnp.dot(z, w1_ref[...], precision=_HP, preferred_element_type=jnp.float32) + b1_ref[...]))
    h = jnp.sin(fr * (jnp.dot(h, w2_ref[...], precision=_HP, preferred_element_type=jnp.float32) + b2_ref[...]))
    h = jnp.sin(fr * (jnp.dot(h, w3_ref[...], precision=_HP, preferred_element_type=jnp.float32) + b3_ref[...]))
    h = jnp.dot(h, wo_ref[...], precision=_HP, preferred_element_type=jnp.float32)
    decay = jnp.exp(-z[:, 0:1] * dl_ref[...])
    hf = h[:, :D_HYENA] * decay
    hb = h[:, D_HYENA:] * decay
    rows = lax.broadcasted_iota(jnp.int32, hb.shape, 0) + pl.program_id(0) * hb.shape[0]
    hb = jnp.where(rows == 0, 0.0, hb)
    gs = hf + hb
    gd = hb - hf
    gs_hi = gs.astype(jnp.bfloat16)
    gd_hi = gd.astype(jnp.bfloat16)
    gs_hi_ref[...] = gs_hi
    gd_hi_ref[...] = gd_hi
    gs_lo_ref[...] = (gs - gs_hi.astype(jnp.float32)).astype(jnp.bfloat16)
    gd_lo_ref[...] = (gd - gd_hi.astype(jnp.float32)).astype(jnp.bfloat16)


def _pad_to(a, shape):
    return jnp.pad(a, [(0, n - s) for s, n in zip(a.shape, shape)])


def _hyena_filter_taps(L, p, tl=512):
    z, deltas = _filter_consts(L)
    P = LANES
    w1 = _pad_to(p["filt_w1"], (P, P))
    w2 = _pad_to(p["filt_w2"], (P, P))
    w3 = _pad_to(p["filt_w3"], (P, P))
    wo = _pad_to(p["filt_w_out"], (P, 2 * D_HYENA))
    b1 = _pad_to(p["filt_b1"].reshape(1, -1), (1, P))
    b2 = _pad_to(p["filt_b2"].reshape(1, -1), (1, P))
    b3 = _pad_to(p["filt_b3"].reshape(1, -1), (1, P))
    fr = _pad_to(p["filt_freq"].reshape(1, -1), (1, P))
    const = lambda i: (0, 0)
    out = jax.ShapeDtypeStruct((L, D_HYENA), jnp.bfloat16)
    return pl.pallas_call(
        _filter_body,
        grid=(L // tl,),
        in_specs=[
            pl.BlockSpec((tl, P), lambda i: (i, 0)),
            pl.BlockSpec((1, D_HYENA), const),
            pl.BlockSpec((P, P), const), pl.BlockSpec((1, P), const),
            pl.BlockSpec((P, P), const), pl.BlockSpec((1, P), const),
            pl.BlockSpec((P, P), const), pl.BlockSpec((1, P), const),
            pl.BlockSpec((P, 2 * D_HYENA), const), pl.BlockSpec((1, P), const),
        ],
        out_specs=[pl.BlockSpec((tl, D_HYENA), lambda i: (i, 0))] * 4,
        out_shape=[out] * 4,
        compiler_params=_params("parallel"),
        name="hyena_filter",
    )(z, deltas, w1, b1, w2, b2, w3, b3, wo, fr)


def _spectrum_body(L, c_ref, s_ref, alt_ref, gsh_ref, gsl_ref, gdh_ref, gdl_ref,
                   p_ref, q_ref, kn_ref):
    c = c_ref[...]
    s = s_ref[...]
    tf = c.shape[0]
    f = lax.broadcasted_iota(jnp.int32, (tf, 1), 0) + pl.program_id(1) * tf
    scale = jnp.where(f == 0, 1.0, 2.0) * (1.0 / (2 * L))
    p_ref[...] = (_bdot(c, gsh_ref[...]) + _bdot(c, gsl_ref[...])) * scale
    q_ref[...] = (_bdot(s, gdh_ref[...]) + _bdot(s, gdl_ref[...])) * scale
    alt = alt_ref[...]
    kn_ref[...] = (_bdot(alt, gsh_ref[...]) + _bdot(alt, gsl_ref[...])) * (1.0 / (2 * L))


def _filter_spectrum(L, taps, tf=512, tc=256):
    cmat, smat, alt = _dft_tables(L)
    gcol = pl.BlockSpec((L, tc), lambda j, k: (0, j))
    return pl.pallas_call(
        functools.partial(_spectrum_body, L),
        grid=(D_HYENA // tc, L // tf),
        in_specs=[
            pl.BlockSpec((tf, L), lambda j, k: (k, 0)),
            pl.BlockSpec((tf, L), lambda j, k: (k, 0)),
            pl.BlockSpec((SUBLANES, L), lambda j, k: (0, 0)),
            gcol, gcol, gcol, gcol,
        ],
        out_specs=[
            pl.BlockSpec((tf, tc), lambda j, k: (k, j)),
            pl.BlockSpec((tf, tc), lambda j, k: (k, j)),
            pl.BlockSpec((SUBLANES, tc), lambda j, k: (0, j)),
        ],
        out_shape=[
            jax.ShapeDtypeStruct((L, D_HYENA), jnp.float32),
            jax.ShapeDtypeStruct((L, D_HYENA), jnp.float32),
            jax.ShapeDtypeStruct((SUBLANES, D_HYENA), jnp.float32),
        ],
        compiler_params=_params("parallel", "arbitrary"),
        name="filter_spectrum",
    )(cmat, smat, alt, *taps)


def _fwd_dft_body(c_ref, s_ref, alt_ref, u_ref, p_ref, q_ref, kn_ref, yr_ref, z2_ref, yn_ref):
    u = u_ref[0]
    a = _bdot(c_ref[...], u)
    b = _bdot(s_ref[...], u)
    p = p_ref[...]
    q = q_ref[...]
    yr_ref[0] = (a * p + b * q).astype(yr_ref.dtype)
    z2_ref[0] = (b * p - a * q).astype(z2_ref.dtype)
    yn_ref[0] = _bdot(alt_ref[...], u) * kn_ref[...]


def _fwd_dft(u_bf16, P, Q, KN, tf, tc):
    B, L, C = u_bf16.shape
    cmat, smat, alt = _dft_tables(L)
    return pl.pallas_call(
        _fwd_dft_body,
        grid=(B, C // tc, L // tf),
        in_specs=[
            pl.BlockSpec((tf, L), lambda b, j, k: (k, 0)),
            pl.BlockSpec((tf, L), lambda b, j, k: (k, 0)),
            pl.BlockSpec((SUBLANES, L), lambda b, j, k: (0, 0)),
            pl.BlockSpec((1, L, tc), lambda b, j, k: (b, 0, j)),
            pl.BlockSpec((tf, tc), lambda b, j, k: (k, j)),
            pl.BlockSpec((tf, tc), lambda b, j, k: (k, j)),
            pl.BlockSpec((SUBLANES, tc), lambda b, j, k: (0, j)),
        ],
        out_specs=[
            pl.BlockSpec((1, tf, tc), lambda b, j, k: (b, k, j)),
            pl.BlockSpec((1, tf, tc), lambda b, j, k: (b, k, j)),
            pl.BlockSpec((1, SUBLANES, tc), lambda b, j, k: (b, 0, j)),
        ],
        out_shape=[
            jax.ShapeDtypeStruct((B, L, C), jnp.bfloat16),
            jax.ShapeDtypeStruct((B, L, C), jnp.bfloat16),
            jax.ShapeDtypeStruct((B, SUBLANES, C), jnp.float32),
        ],
        compiler_params=_params("parallel", "parallel", "arbitrary"),
        name="fwd_dft",
    )(cmat, smat, alt, u_bf16, P, Q, KN)


def _inv_dft_body(c_ref, s_ref, yr_ref, z2_ref, yn_ref, x0_ref, u_ref, skip_ref, o_ref):
    y = _bdot(c_ref[...], yr_ref[0]) + _bdot(s_ref[...], z2_ref[0])
    tt = y.shape[0]
    t = lax.broadcasted_iota(jnp.int32, (tt, 1), 0) + pl.program_id(2) * tt
    sign = (1 - 2 * (t & 1)).astype(jnp.float32)
    y = y + sign * yn_ref[0, 0:1, :]
    o_ref[0] = (x0_ref[0] * (y + u_ref[0] * skip_ref[...])).astype(o_ref.dtype)


def _inv_dft(yr, z2, yn, x0, u, skip, tt, tc):
    B, L, C = yr.shape
    cmat, smat, _ = _dft_tables(L)
    return pl.pallas_call(
        _inv_dft_body,
        grid=(B, C // tc, L // tt),
        in_specs=[
            pl.BlockSpec((tt, L), lambda b, j, k: (k, 0)),
            pl.BlockSpec((tt, L), lambda b, j, k: (k, 0)),
            pl.BlockSpec((1, L, tc), lambda b, j, k: (b, 0, j)),
            pl.BlockSpec((1, L, tc), lambda b, j, k: (b, 0, j)),
            pl.BlockSpec((1, SUBLANES, tc), lambda b, j, k: (b, 0, j)),
            pl.BlockSpec((1, tt, tc), lambda b, j, k: (b, k, j)),
            pl.BlockSpec((1, tt, tc), lambda b, j, k: (b, k, j)),
            pl.BlockSpec((1, tc), lambda b, j, k: (0, j)),
        ],
        out_specs=pl.BlockSpec((1, tt, tc), lambda b, j, k: (b, k, j)),
        out_shape=jax.ShapeDtypeStruct((B, L, C), jnp.bfloat16),
        compiler_params=_params("parallel", "parallel", "arbitrary"),
        name="inv_dft",
    )(cmat, smat, yr, z2, yn, x0, u, skip.reshape(1, C))


ROW_SPLIT_WIDTH = LANES


def _load_row_split(ref, n, rpt):
    return jnp.concatenate([ref[pl.ds(k, n, stride=rpt), :] for k in range(rpt)], axis=1)


def _branch_merge_body(ya_ref, yb_ref, ga_ref, gb_ref, wa_ref, wh_ref, m_ref):
    pa = _bdot(ya_ref[...], wa_ref[...])
    ph = _bdot(yb_ref[...], wh_ref[...])
    merged = jax.nn.sigmoid(ga_ref[...]) * pa + jax.nn.sigmoid(gb_ref[...]) * ph
    m_ref[...] = merged.astype(m_ref.dtype)


def _branch_merge(ya, yb, proj, wa, wh, tm=512):
    T = ya.shape[0]
    D = wa.shape[1]
    const = lambda i: (0, 0)
    row = lambda i: (i, 0)
    one = pl.Buffered(1)
    ga_blk = G_OFF // D
    return pl.pallas_call(
        _branch_merge_body,
        grid=(T // tm,),
        in_specs=[
            pl.BlockSpec((tm, D_CONV), row),
            pl.BlockSpec((tm, D_HYENA), row),
            pl.BlockSpec((tm, D), lambda i: (i, ga_blk)),
            pl.BlockSpec((tm, D), lambda i: (i, ga_blk + 1)),
            pl.BlockSpec((D_CONV, D), const, pipeline_mode=one),
            pl.BlockSpec((D_HYENA, D), const, pipeline_mode=one),
        ],
        out_specs=pl.BlockSpec((tm, D), row),
        out_shape=jax.ShapeDtypeStruct((T, D), jnp.bfloat16),
        compiler_params=_params("parallel"),
        name="branch_merge",
    )(ya, yb, proj, proj, wa, wh)


def _merge_body(m_ref, x_ref, wo_ref, gn_ref, wrt_ref, h_ref, xn_ref, aff_ref):
    h = x_ref[...] + _bdot(m_ref[...], wo_ref[...])
    h_ref[...] = h
    xn = _rms(h, gn_ref[...])
    xn_ref[...] = xn
    lg = lax.dot_general(wrt_ref[...], xn, (((1,), (1,)), ((), ())),
                         preferred_element_type=jnp.float32, precision=_HP)
    ex = jnp.exp(lg - jnp.max(lg, axis=0, keepdims=True))
    aff_ref[...] = ex / jnp.sum(ex, axis=0, keepdims=True)


def _merge_out(merged, x, wo, g_ffn, w_router_t, tm=512):
    T, D = x.shape
    const = lambda i: (0, 0)
    row = lambda i: (i, 0)
    one = pl.Buffered(1)
    return pl.pallas_call(
        _merge_body,
        grid=(T // tm,),
        in_specs=[
            pl.BlockSpec((tm, D), row),
            pl.BlockSpec((tm, D), row),
            pl.BlockSpec((D, D), const, pipeline_mode=one),
            pl.BlockSpec((1, D), const, pipeline_mode=one),
            pl.BlockSpec((N_EXPERTS, D), const, pipeline_mode=one),
        ],
        out_specs=[
            pl.BlockSpec((tm, D), row),
            pl.BlockSpec((tm, D), row),
            pl.BlockSpec((N_EXPERTS, tm), lambda i: (0, i)),
        ],
        out_shape=[
            jax.ShapeDtypeStruct((T, D), jnp.float32),
            jax.ShapeDtypeStruct((T, D), jnp.float32),
            jax.ShapeDtypeStruct((N_EXPERTS, T), jnp.float32),
        ],
        compiler_params=_params("parallel"),
        name="merge_out",
    )(merged, x, wo, g_ffn.reshape(1, D), w_router_t)


def _route_body(cap, aff_ref, idx_ref, g_ref, starts_ref):
    f32, bf16 = jnp.float32, jnp.bfloat16
    x = aff_ref[0]
    R = x.shape[0]
    bits = pltpu.bitcast(x, jnp.int32)

    def count(m):
        return jnp.sum(jnp.where(m, 1.0, 0.0), keepdims=True)

    def bit_step(i, cur):
        cand = cur | jnp.left_shift(jnp.int32(1), 30 - i)
        return jnp.where(count(bits >= cand) >= cap, cand, cur)

    thr = lax.fori_loop(0, 31, bit_step, jnp.zeros((1, 1), jnp.int32))

    def tri(n, cmp):
        a = lax.broadcasted_iota(jnp.int32, (n, n), 0)
        b = lax.broadcasted_iota(jnp.int32, (n, n), 1)
        return jnp.where(cmp(a, b), 1.0, 0.0).astype(bf16)

    upper = tri(LANES, lambda k, l: k <= l)
    ones = jnp.ones((LANES, LANES), bf16)
    strict_lower = tri(R, lambda r, k: k < r)

    def prefix(m01):
        mb = m01.astype(bf16)
        rowcum = _bdot(mb, upper)
        rowtot = _bdot(mb, ones)
        before = _bdot(strict_lower, rowtot.astype(bf16))
        return mb, rowcum, rowtot, before

    gt = jnp.where(bits > thr, 1.0, 0.0)
    eq = jnp.where(bits == thr, 1.0, 0.0)
    _, eq_cum, _, eq_before = prefix(eq)
    need = cap - jnp.sum(gt, keepdims=True)
    sel = gt + eq * jnp.where(eq_cum + eq_before <= need, 1.0, 0.0)
    selb, rowcum, rowtot, before = prefix(sel)

    through_row = (before + rowtot)[:, 0:1]
    j = lax.broadcasted_iota(jnp.int32, (1, cap), 1).astype(f32)
    done = through_row <= j
    r_j = jnp.sum(jnp.where(done, 1.0, 0.0), axis=0, keepdims=True)
    before_j = jnp.sum(jnp.where(done, rowtot[:, 0:1], 0.0), axis=0, keepdims=True)
    rows = lax.broadcasted_iota(jnp.int32, (R, cap), 0).astype(f32)
    pick_row = jnp.where(rows == r_j, 1.0, 0.0).astype(bf16)

    nt = (((1,), (1,)), ((), ()))
    lower = tri(LANES, lambda l, k: k <= l)
    rowcum_t = lax.dot_general(lower, selb, nt, preferred_element_type=f32)
    cum_j = _bdot(rowcum_t.astype(bf16), pick_row)
    lane_j = jnp.sum(jnp.where(cum_j <= j - before_j, 1.0, 0.0), axis=0, keepdims=True)
    idx_ref[0] = (r_j * LANES + lane_j).astype(jnp.int32)

    rowtot_t = lax.dot_general(jnp.ones((SUBLANES, LANES), bf16), selb, nt, preferred_element_type=f32)
    strict_upper = tri(R, lambda k, r: k < r)
    starts_ref[0] = _bdot(rowtot_t.astype(bf16), strict_upper).astype(jnp.int32)

    eye = tri(LANES, lambda a, b: a == b)
    hi = x.astype(bf16)
    r1 = x - hi.astype(f32)
    mid = r1.astype(bf16)
    lo = (r1 - mid.astype(f32)).astype(bf16)

    def pick(part):
        part_t = lax.dot_general(eye, part, nt, preferred_element_type=f32)
        return _bdot(part_t.astype(bf16), pick_row)

    aff_j = (pick(hi) + pick(mid)) + pick(lo)
    lanes = lax.broadcasted_iota(jnp.int32, (LANES, cap), 0).astype(f32)
    g_ref[0] = jnp.sum(jnp.where(lanes == lane_j, aff_j, 0.0), axis=0, keepdims=True)


def _route(aff_t, cap):
    E, T = aff_t.shape
    R = T // LANES
    return pl.pallas_call(
        functools.partial(_route_body, cap),
        grid=(E,),
        in_specs=[pl.BlockSpec((1, R, LANES), lambda e: (e, 0, 0))],
        out_specs=[pl.BlockSpec((1, 1, cap), lambda e: (e, 0, 0)),
                   pl.BlockSpec((1, 1, cap), lambda e: (e, 0, 0)),
                   pl.BlockSpec((1, SUBLANES, R), lambda e: (e, 0, 0))],
        out_shape=[jax.ShapeDtypeStruct((E, 1, cap), jnp.int32),
                   jax.ShapeDtypeStruct((E, 1, cap), jnp.float32),
                   jax.ShapeDtypeStruct((E, SUBLANES, R), jnp.int32)],
        compiler_params=_params("parallel"),
        name="route",
    )(aff_t.reshape(E, R, LANES))


_SEM_X, _SEM_OUT, _SEM_IDX = range(3)


def _moe_body(tm, rpt, n_tiles, n_f, idx_hbm, g_ref, wg_ref, wu_ref, wd_ref, xn_hbm, y_hbm,
              idx_smem, xrows, yrows, xb, acc, sem):
    n = pl.program_id(0) * pl.num_programs(1) + pl.program_id(1)
    f = pl.program_id(2)
    last = n_f - 1
    share = xrows.shape[0] // n_f
    next_tile = jnp.minimum(n + 1, n_tiles - 1)

    def load_idx(tile):
        cp = pltpu.make_async_copy(idx_hbm.at[tile], idx_smem, sem.at[_SEM_IDX])
        cp.start()
        cp.wait()

    def start_row(j):
        t = idx_smem[jnp.minimum(j, tm - 1)]
        pltpu.make_async_copy(xn_hbm.at[pl.ds(t, 1), :], xrows.at[pl.ds(j, 1), :], sem.at[_SEM_X]).start()

    x_landed = pltpu.make_async_copy(xn_hbm.at[pl.ds(0, xrows.shape[0]), :], xrows, sem.at[_SEM_X])
    tile_rows = pl.ds(pl.multiple_of(n * (tm * rpt), tm * rpt), tm * rpt)
    y_out = pltpu.make_async_copy(yrows, y_hbm.at[tile_rows, :], sem.at[_SEM_OUT])

    @pl.when(f == 0)
    def _():
        @pl.when(n == 0)
        def _():
            load_idx(0)

            def body(j, c):
                start_row(j)
                return c
            lax.fori_loop(0, xrows.shape[0], body, 0, unroll=8)

        x_landed.wait()
        xb[...] = xrows[pl.ds(0, tm), :].astype(jnp.bfloat16)
        acc[...] = jnp.zeros_like(acc)
        load_idx(next_tile)

    for u in range(share):
        start_row(f * share + u)

    x = xb[...]
    a = _bdot(x, wg_ref[0].astype(jnp.bfloat16))
    b = _bdot(x, wu_ref[0].astype(jnp.bfloat16))
    hmid = (a * jax.nn.sigmoid(a) * b).astype(jnp.bfloat16)
    acc[...] += _bdot(hmid, wd_ref[0].astype(jnp.bfloat16))

    @pl.when(f == last)
    def _():
        @pl.when(n > 0)
        def _():
            y_out.wait()

        g = g_ref[0]
        for k in range(rpt):
            yrows[pl.ds(k, tm, stride=rpt), :] = acc[:, k * LANES:(k + 1) * LANES] * g
        y_out.start()

        @pl.when(n == n_tiles - 1)
        def _():
            y_out.wait()
            x_landed.wait()


def _moe(idx, g, xn, w_gate, w_up, w_down, tf=256):
    E, _, C = idx.shape
    D, F = w_gate.shape[1], w_gate.shape[2]
    rpt = D // ROW_SPLIT_WIDTH
    tm = min(C, 1024)
    any_spec = pl.BlockSpec(memory_space=pl.ANY)
    n_tiles = E * (C // tm)
    n_f = F // tf
    gather_rows = -(-tm // (n_f * SUBLANES)) * SUBLANES * n_f
    return pl.pallas_call(
        functools.partial(_moe_body, tm, rpt, n_tiles, n_f),
        grid=(E, C // tm, n_f),
        in_specs=[
            any_spec,
            pl.BlockSpec((1, tm, 1), lambda e, i, f: (e, i, 0)),
            pl.BlockSpec((1, D, tf), lambda e, i, f: (e, 0, f)),
            pl.BlockSpec((1, D, tf), lambda e, i, f: (e, 0, f)),
            pl.BlockSpec((1, tf, D), lambda e, i, f: (e, f, 0)),
            any_spec,
        ],
        out_specs=any_spec,
        out_shape=jax.ShapeDtypeStruct((E * C * rpt, ROW_SPLIT_WIDTH), jnp.float32),
        scratch_shapes=[
            pltpu.SMEM((tm,), jnp.int32),
            pltpu.VMEM((gather_rows, D), jnp.float32),
            pltpu.VMEM((tm * rpt, ROW_SPLIT_WIDTH), jnp.float32),
            pltpu.VMEM((tm, D), jnp.bfloat16),
            pltpu.VMEM((tm, D), jnp.float32),
            pltpu.SemaphoreType.DMA((3,)),
        ],
        compiler_params=pltpu.CompilerParams(
            dimension_semantics=("arbitrary", "arbitrary", "arbitrary"),
            vmem_limit_bytes=VMEM_LIMIT_BYTES, has_side_effects=True),
        name="moe",
    )(idx.reshape(n_tiles, tm), g.reshape(E, C, 1), w_gate, w_up, w_down, xn)


_COMBINE_UNROLL = 4


def _combine_body(tt, rpt, cap, n_exp, n_rows, ch, idx_ref, starts_ref, h_ref, y_hbm, gn_ref, o_ref,
                  ybuf, acc, sem):
    i = pl.program_id(0)
    slot = i % 2
    t0 = i * tt
    total = n_exp * cap
    rows_per_tile = tt // LANES

    def lower_bound(e, row):
        s = starts_ref[e * n_rows + jnp.minimum(row, n_rows - 1)]
        return jnp.where(row < n_rows, s, cap)

    def run_of(e, tile):
        a = lower_bound(e, tile * rows_per_tile)
        b = lower_bound(e, (tile + 1) * rows_per_tile)
        return e * cap + a, b - a

    def chunk_copy(e, first_slot, s):
        start = jnp.minimum(first_slot, total - ch)
        src = y_hbm.at[pl.ds(pl.multiple_of(start * rpt, rpt), ch * rpt), :]
        return pltpu.make_async_copy(src, ybuf.at[s, e], sem.at[s, e]), first_slot - start

    def fetch_first_chunks(tile, s):
        for e in range(n_exp):
            chunk_copy(e, run_of(e, tile)[0], s)[0].start()

    @pl.when(i == 0)
    def _():
        fetch_first_chunks(0, 0)

    @pl.when(i + 1 < pl.num_programs(0))
    def _():
        fetch_first_chunks(i + 1, 1 - slot)

    acc[...] = jnp.zeros_like(acc)

    def add_rows(e, first_slot, off, m):
        def token_rows(r):
            t = idx_ref[first_slot + r] - t0
            return pl.ds(pl.multiple_of(t * rpt, rpt), rpt)

        def y_rows(r):
            return pl.ds(pl.multiple_of((off + r) * rpt, rpt), rpt)

        def group(q, c):
            r0 = q * _COMBINE_UNROLL
            dst = [token_rows(r0 + u) for u in range(_COMBINE_UNROLL)]
            sums = [acc[dst[u], :] + ybuf[slot, e, y_rows(r0 + u), :] for u in range(_COMBINE_UNROLL)]
            for u in range(_COMBINE_UNROLL):
                acc[dst[u], :] = sums[u]
            return c

        full = m // _COMBINE_UNROLL
        lax.fori_loop(0, full, group, 0)

        def single(r, c):
            d = token_rows(r)
            acc[d, :] = acc[d, :] + ybuf[slot, e, y_rows(r), :]
            return c

        lax.fori_loop(full * _COMBINE_UNROLL, m, single, 0)

    for e in range(n_exp):
        first_slot, count = run_of(e, i)
        cp, off = chunk_copy(e, first_slot, slot)
        cp.wait()
        m0 = jnp.minimum(count, ch - off)
        add_rows(e, first_slot, off, m0)

        def more(done):
            cp2, off2 = chunk_copy(e, first_slot + done, slot)
            cp2.start()
            cp2.wait()
            m = jnp.minimum(count - done, ch - off2)
            add_rows(e, first_slot + done, off2, m)
            return done + m

        lax.while_loop(lambda done: done < count, more, m0)

    o_ref[...] = _rms(h_ref[...] + _load_row_split(acc, tt, rpt), gn_ref[...])


def _combine_norm(idx, starts, h, y_rs, g_final, tt=256, ch=64):
    E, _, C = idx.shape
    T, D = h.shape
    rpt = D // ROW_SPLIT_WIDTH
    n_rows = T // LANES
    grid_spec = pltpu.PrefetchScalarGridSpec(
        num_scalar_prefetch=2,
        grid=(T // tt,),
        in_specs=[
            pl.BlockSpec((tt, D), lambda i, *_: (i, 0)),
            pl.BlockSpec(memory_space=pl.ANY),
            pl.BlockSpec((1, D), lambda i, *_: (0, 0)),
        ],
        out_specs=pl.BlockSpec((tt, D), lambda i, *_: (i, 0)),
        scratch_shapes=[
            pltpu.VMEM((2, E, ch * rpt, ROW_SPLIT_WIDTH), jnp.float32),
            pltpu.VMEM((tt * rpt, ROW_SPLIT_WIDTH), jnp.float32),
            pltpu.SemaphoreType.DMA((2, E)),
        ],
    )
    return pl.pallas_call(
        functools.partial(_combine_body, tt, rpt, C, E, n_rows, ch),
        grid_spec=grid_spec,
        out_shape=jax.ShapeDtypeStruct((T, D), jnp.float32),
        compiler_params=_params("arbitrary"),
        name="combine_norm",
    )(idx.reshape(E * C), starts[:, 0, :].reshape(E * n_rows), h, y_rs, g_final.reshape(1, D))


def _trunk(x, p):
    B, L, D = x.shape
    T = B * L
    xf = x.reshape(T, D)
    proj = _norm_proj(xf, p["norm_mix"], p["w_in"])
    ya, x0, u, ub = _mix_front(proj, p["conv_a_w"], p["conv_h_w"], L)

    taps = _hyena_filter_taps(L, p)
    P, Q, KN = _filter_spectrum(L, taps)
    tc = 1024 if L <= 2048 else 512
    shp = (B, L, D_HYENA)
    yr, z2, yn = _fwd_dft(ub.reshape(shp), P, Q, KN, tf=512, tc=tc)
    yb = _inv_dft(yr, z2, yn, x0.reshape(shp), u.reshape(shp), p["hyena_skip"], tt=512, tc=tc)

    merged = _branch_merge(ya, yb.reshape(T, D_HYENA), proj, p["w_proj_a"], p["w_proj_h"])
    h, xn, aff_t = _merge_out(merged, xf, p["w_out"], p["norm_ffn"], p["w_router"].T)

    cap = max(1, CAPACITY_FACTOR * T // N_EXPERTS)
    idx, g, starts = _route(aff_t, cap)
    y_rs = _moe(idx, g, xn, p["w_gate"], p["w_up"], p["w_down"])
    return _combine_norm(idx, starts, h, y_rs, p["norm_final"]).reshape(B, L, D)


def kernel(x_prompt, x_sample, w_in, conv_a_w, conv_h_w, filt_w1, filt_b1, filt_w2, filt_b2, filt_w3, filt_b3, filt_w_out, filt_freq, hyena_skip, w_proj_a, w_proj_h, w_out, norm_mix, norm_ffn, w_router, w_gate, w_up, w_down, norm_final):
    bf16 = jnp.bfloat16
    p = dict(
        w_in=w_in[0].astype(bf16), conv_a_w=conv_a_w[0], conv_h_w=conv_h_w[0],
        filt_w1=filt_w1[0], filt_b1=filt_b1[0], filt_w2=filt_w2[0], filt_b2=filt_b2[0],
        filt_w3=filt_w3[0], filt_b3=filt_b3[0], filt_w_out=filt_w_out[0], filt_freq=filt_freq[0],
        hyena_skip=hyena_skip[0], w_proj_a=w_proj_a[0].astype(bf16), w_proj_h=w_proj_h[0].astype(bf16),
        w_out=w_out[0].astype(bf16), norm_mix=norm_mix[0], norm_ffn=norm_ffn[0],
        w_router=w_router[0], w_gate=w_gate[0], w_up=w_up[0], w_down=w_down[0],
        norm_final=norm_final,
    )
    return (_trunk(x_prompt, p), _trunk(x_sample, p))
```
